```python
import jax, jax.numpy as jnp
from jax import lax
import numpy as np

D_MODEL = 4096
BATCH = 2
SEQ = 8192
DEPTH = 2

A_GROUPS = 16
A_GROUP_DIM = 128
A_WIDTH = A_GROUPS * A_GROUP_DIM
A_CHUNK = 128
B_HEADS = 16
B_HEAD_DIM = 128
B_WIDTH = B_HEADS * B_HEAD_DIM
B_CONV = 5
B_CHUNK = 64
C_HEADS = 16
C_Q_LORA = 1024
C_KV_LORA = 512
C_NOPE = 128
C_ROPE = 64
C_V = 128
C_WIDTH = C_HEADS * C_V
C_QBLOCK = 128
ROPE_THETA = 10000.0
N_BRANCH = 3
BRANCH_WIDTH = 2048
D_FF = 11008
EPS = 1e-6

IN_SPLITS = (A_WIDTH, A_WIDTH, 3 * B_WIDTH, B_WIDTH, 2 * B_HEADS, 2 * B_HEADS, C_Q_LORA, C_KV_LORA, C_ROPE)
IN_WIDTH = sum(IN_SPLITS)

kernel_name = "hybrid_gmlp_gdn_mla_macaron_encoder"


def rmsnorm(x, gain):
    xf = x.astype(jnp.float32)
    y = xf * lax.rsqrt(jnp.mean(xf * xf, axis=-1, keepdims=True) + EPS)
    return (y * gain.astype(jnp.float32)).astype(x.dtype)


def l2norm(t):
    tf = t.astype(jnp.float32)
    return tf * lax.rsqrt(jnp.sum(tf * tf, axis=-1, keepdims=True) + EPS)


def swiglu(x, w_up, w_down):
    gate, up = jnp.split(x @ w_up, 2, axis=-1)
    return (jax.nn.silu(gate) * up) @ w_down


def rope(t, cos, sin):
    t1, t2 = jnp.split(t, 2, axis=-1)
    return jnp.concatenate([t1 * cos - t2 * sin, t1 * sin + t2 * cos], axis=-1)


def chunked_spatial_gating(u, v, v_gain, w_s, b_s):
    Bb, S, _ = v.shape
    n = S // A_CHUNK
    v = rmsnorm(v, v_gain).reshape(Bb, n, A_CHUNK, A_GROUPS, A_GROUP_DIM)
    s = jnp.einsum('gpq,bnqgc->bnpgc', w_s, v) + jnp.swapaxes(b_s, 0, 1)[None, None, :, :, None]
    return u * s.reshape(Bb, S, A_WIDTH)


def centred_depthwise_conv(x, w):
    pad = (B_CONV - 1) // 2
    return lax.conv_general_dilated(
        x, w[:, None, :].astype(x.dtype), window_strides=(1,), padding=[(pad, pad)],
        dimension_numbers=('NWC', 'WIO', 'NWC'), feature_group_count=x.shape[-1])


def gated_delta_rule_chunked(q, k, v, g, beta):
    Bb, S, H, Dk = q.shape
    Dv = v.shape[-1]
    n = S // B_CHUNK

    def chunks(t):
        t = t.astype(jnp.float32).reshape((Bb, n, B_CHUNK, H) + t.shape[3:])
        return jnp.moveaxis(jnp.moveaxis(t, 1, 0), 3, 2)

    q = chunks(q) * (Dk ** -0.5)
    k, v, g, beta = chunks(k), chunks(v), chunks(g), chunks(beta)
    g = jnp.cumsum(g, axis=-1)
    incl = jnp.tril(jnp.ones((B_CHUNK, B_CHUNK), dtype=bool))
    strict = jnp.tril(jnp.ones((B_CHUNK, B_CHUNK), dtype=bool), -1)
    decay = jnp.exp(jnp.where(incl, g[..., :, None] - g[..., None, :], -jnp.inf))
    k_beta = k * beta[..., None]
    kk = jnp.einsum('nbhcd,nbhjd->nbhcj', k_beta, k) * decay
    lower = jnp.eye(B_CHUNK, dtype=jnp.float32) + jnp.where(strict, kk, 0.0)
    rhs = jnp.concatenate([v * beta[..., None], k_beta * jnp.exp(g)[..., None]], axis=-1)
    uw = lax.linalg.triangular_solve(lower, rhs, left_side=True, lower=True, unit_diagonal=True)
    u, w = uw[..., :Dv], uw[..., Dv:]
    attn = jnp.einsum('nbhcd,nbhjd->nbhcj', q, k) * decay
    q_dec = q * jnp.exp(g)[..., None]
    g_last = g[..., -1]
    k_dec = k * jnp.exp(g_last[..., None] - g)[..., None]

    def step(state, xs):
        u_i, w_i, attn_i, q_i, k_i, gl_i = xs
        v_new = u_i - jnp.einsum('bhcd,bhde->bhce', w_i, state)
        o_i = jnp.einsum('bhcd,bhde->bhce', q_i, state) + jnp.einsum('bhcj,bhje->bhce', attn_i, v_new)
        state = state * jnp.exp(gl_i)[..., None, None] + jnp.einsum('bhcd,bhce->bhde', k_i, v_new)
        return state, o_i

    state0 = jnp.zeros((Bb, H, Dk, Dv), jnp.float32)
    _, o = lax.scan(step, state0, (u, w, attn, q_dec, k_dec, g_last))
    return jnp.moveaxis(o, 0, 1).transpose(0, 1, 3, 2, 4).reshape(Bb, S, H, Dv)


def gated_deltanet_bidir(qkv, z, a, b, conv_w, a_log, dt_bias, o_gain):
    Bb, S, _ = qkv.shape
    qkv = jax.nn.silu(centred_depthwise_conv(qkv, conv_w))
    q, k, v = jnp.split(qkv, 3, axis=-1)
    q = l2norm(q.reshape(Bb, S, B_HEADS, B_HEAD_DIM))
    k = l2norm(k.reshape(Bb, S, B_HEADS, B_HEAD_DIM))
    v = v.reshape(Bb, S, B_HEADS, B_HEAD_DIM)
    a = a.astype(jnp.float32).reshape(Bb, S, 2, B_HEADS)
    g = -jnp.exp(a_log.astype(jnp.float32)) * jax.nn.softplus(a + dt_bias.astype(jnp.float32))
    beta = jax.nn.sigmoid(b.astype(jnp.float32).reshape(Bb, S, 2, B_HEADS))
    o_fwd = gated_delta_rule_chunked(q, k, v, g[:, :, 0], beta[:, :, 0])
    flip = lambda t: jnp.flip(t, axis=1)
    o_bwd = flip(gated_delta_rule_chunked(flip(q), flip(k), flip(v), flip(g[:, :, 1]), flip(beta[:, :, 1])))
    o = rmsnorm(o_fwd + o_bwd, o_gain) * jax.nn.silu(z.astype(jnp.float32).reshape(Bb, S, B_HEADS, B_HEAD_DIM))
    return o.reshape(Bb, S, B_WIDTH)


def mla_attention(q_nope, q_rope, k_nope, k_rope, v):
    Bb, S, H, _ = q_nope.shape
    nb = S // C_QBLOCK
    scale = (C_NOPE + C_ROPE) ** -0.5

    def blocks(t):
        return jnp.moveaxis(t.reshape((Bb, nb, C_QBLOCK) + t.shape[2:]), 1, 0)

    def attend(qb):
        qn, qr = qb
        s = (jnp.einsum('bqhd,bkhd->bhqk', qn, k_nope, preferred_element_type=jnp.float32)
             + jnp.einsum('bqhr,bkr->bhqk', qr, k_rope, preferred_element_type=jnp.float32)) * scale
        p = jax.nn.softmax(s, axis=-1)
        return jnp.einsum('bhqk,bkhd->bqhd', p.astype(v.dtype), v)

    o = lax.map(attend, (blocks(q_nope), blocks(q_rope)))
    return jnp.moveaxis(o, 0, 1).reshape(Bb, S, H * C_V)


def hybrid_mixer(h, cos, sin, w_in, w_gate, a_v_gain, a_w_s, a_b_s, b_conv, b_a_log, b_dt_bias,
                 b_o_gain, c_q_gain, c_kv_gain, c_w_uq, c_w_ukv, w_branch, w_o):
    Bb, S, _ = h.shape
    offsets = np.cumsum(IN_SPLITS)[:-1].tolist()
    a_u, a_v, b_qkv, b_z, b_a, b_b, c_dq, c_dkv, c_kr = jnp.split(h @ w_in, offsets, axis=-1)

    y_a = chunked_spatial_gating(jax.nn.gelu(a_u, approximate=False), jax.nn.gelu(a_v, approximate=False),
                                 a_v_gain, a_w_s, a_b_s)

    y_b = gated_deltanet_bidir(b_qkv, b_z, b_a, b_b, b_conv, b_a_log, b_dt_bias, b_o_gain)

    cq = rmsnorm(c_dq, c_q_gain)
    q = (cq @ c_w_uq).reshape(Bb, S, C_HEADS, C_NOPE + C_ROPE)
    q_nope = q[..., :C_NOPE]
    q_rope = rope(q[..., C_NOPE:], cos[:, :, None, :], sin[:, :, None, :])
    ckv = rmsnorm(c_dkv, c_kv_gain)
    kv = (ckv @ c_w_ukv).reshape(Bb, S, C_HEADS, C_NOPE + C_V)
    k_nope, v = kv[..., :C_NOPE], kv[..., C_NOPE:]
    k_rope = rope(c_kr, cos, sin)
    y_c = mla_attention(q_nope, q_rope, k_nope, k_rope, v)

    ys = jnp.stack([y_a.astype(h.dtype), y_b.astype(h.dtype), y_c.astype(h.dtype)], axis=2)
    proj = jnp.einsum('bsnw,nwd->bsnd', ys, w_branch)
    gates = jax.nn.sigmoid(h @ w_gate).reshape(Bb, S, N_BRANCH, D_MODEL)
    return jnp.sum(gates * proj, axis=2) @ w_o


def setup_inputs(seed: int = 0) -> dict:
    key = jax.random.key(seed)
    ks = iter(jax.random.split(key, 32))

    def normal(shape, fan_in):
        return jax.random.normal(next(ks), shape, jnp.float32) * (fan_in ** -0.5)

    def gain(shape):
        return 1.0 + 0.1 * jax.random.normal(next(ks), shape, jnp.float32)

    x = jax.random.normal(next(ks), (BATCH, SEQ, D_MODEL), jnp.float32)
    positions = (jax.random.randint(next(ks), (BATCH, 1), 0, 4096, dtype=jnp.int32)
                 + jnp.arange(SEQ, dtype=jnp.int32)[None, :])
    b_a_log = jnp.log(jax.random.uniform(next(ks), (DEPTH, 2, B_HEADS), jnp.float32, 1.0, 16.0))
    dt = jnp.exp(jax.random.uniform(next(ks), (DEPTH, 2, B_HEADS), jnp.float32,
                                    float(np.log(1e-3)), float(np.log(1e-1))))
    b_dt_bias = dt + jnp.log(-jnp.expm1(-dt))
    return {
        "x": x,
        "positions": positions,
        "norm_ffn1": gain((DEPTH, D_MODEL)),
        "ffn1_up": normal((DEPTH, D_MODEL, 2 * D_FF), D_MODEL),
        "ffn1_down": normal((DEPTH, D_FF, D_MODEL), D_FF),
        "norm_mix": gain((DEPTH, D_MODEL)),
        "w_in": normal((DEPTH, D_MODEL, IN_WIDTH), D_MODEL),
        "w_gate": normal((DEPTH, D_MODEL, N_BRANCH * D_MODEL), D_MODEL),
        "a_v_gain": gain((DEPTH, A_WIDTH)),
        "a_w_s": normal((DEPTH, A_GROUPS, A_CHUNK, A_CHUNK), A_CHUNK),
        "a_b_s": gain((DEPTH, A_GROUPS, A_CHUNK)),
        "b_conv": normal((DEPTH, B_CONV, 3 * B_WIDTH), B_CONV),
        "b_a_log": b_a_log,
        "b_dt_bias": b_dt_bias,
        "b_o_gain": gain((DEPTH, B_HEAD_DIM)),
        "c_q_gain": gain((DEPTH, C_Q_LORA)),
        "c_kv_gain": gain((DEPTH, C_KV_LORA)),
        "c_w_uq": normal((DEPTH, C_Q_LORA, C_HEADS * (C_NOPE + C_ROPE)), C_Q_LORA),
        "c_w_ukv": normal((DEPTH, C_KV_LORA, C_HEADS * (C_NOPE + C_V)), C_KV_LORA),
        "w_branch": normal((DEPTH, N_BRANCH, BRANCH_WIDTH, D_MODEL), BRANCH_WIDTH),
        "w_o": normal((DEPTH, D_MODEL, D_MODEL), D_MODEL),
        "norm_ffn2": gain((DEPTH, D_MODEL)),
        "ffn2_up": normal((DEPTH, D_MODEL, 2 * D_FF), D_MODEL),
        "ffn2_down": normal((DEPTH, D_FF, D_MODEL), D_FF),
        "norm_final": gain((D_MODEL,)),
    }


def reference(x, positions, norm_ffn1, ffn1_up, ffn1_down, norm_mix, w_in, w_gate, a_v_gain, a_w_s,
              a_b_s, b_conv, b_a_log, b_dt_bias, b_o_gain, c_q_gain, c_kv_gain, c_w_uq, c_w_ukv,
              w_branch, w_o, norm_ffn2, ffn2_up, ffn2_down, norm_final):
    inv_freq = ROPE_THETA ** (-jnp.arange(0, C_ROPE, 2, dtype=jnp.float32) / C_ROPE)
    angles = positions.astype(jnp.float32)[..., None] * inv_freq
    cos = jnp.cos(angles).astype(x.dtype)
    sin = jnp.sin(angles).astype(x.dtype)
    for l in range(DEPTH):
        x = x + 0.5 * swiglu(rmsnorm(x, norm_ffn1[l]), ffn1_up[l], ffn1_down[l])
        h = rmsnorm(x, norm_mix[l])
        x = x + hybrid_mixer(h, cos, sin, w_in[l], w_gate[l], a_v_gain[l], a_w_s[l], a_b_s[l], b_conv[l],
                             b_a_log[l], b_dt_bias[l], b_o_gain[l], c_q_gain[l], c_kv_gain[l], c_w_uq[l],
                             c_w_ukv[l], w_branch[l], w_o[l])
        x = x + 0.5 * swiglu(rmsnorm(x, norm_ffn2[l]), ffn2_up[l], ffn2_down[l])
    return rmsnorm(x, norm_final)
```

```python
import functools
import math

import jax
import jax.numpy as jnp
from jax import lax
from jax.experimental import pallas as pl
from jax.experimental.pallas import tpu as pltpu

F32 = jnp.float32
BF16 = jnp.bfloat16
EPS = 1e-6
ROPE_THETA = 10000.0

A_GROUPS, A_GROUP_DIM, A_CHUNK = 16, 128, 128
A_WIDTH = A_GROUPS * A_GROUP_DIM
B_HEADS, B_HEAD_DIM, B_CONV, B_CHUNK = 16, 128, 5, 64
B_WIDTH = B_HEADS * B_HEAD_DIM
C_HEADS, C_Q_LORA, C_KV_LORA, C_NOPE, C_ROPE, C_V = 16, 1024, 512, 128, 64, 128
C_WIDTH = C_HEADS * C_V
C_QK_PAD = 256
N_BRANCH = 3

VMEM_BUDGET_BYTES = 44 * 1024 * 1024
VMEM_LIMIT_CAP_BYTES = 56 * 1024 * 1024
LANE = 128
SUBLANE = 8
HIGHEST = lax.Precision.HIGHEST


def _pick(n, pref, align):
    if n <= pref:
        return n
    t = (pref // align) * align
    while t >= align:
        if n % t == 0:
            return t
        t -= align
    return n


def _params(sem, vmem_bytes):
    limit = int(min(VMEM_LIMIT_CAP_BYTES, max(vmem_bytes + (8 << 20), 32 << 20)))
    return pltpu.CompilerParams(dimension_semantics=sem, vmem_limit_bytes=limit)


def _dot(a, b):
    return jnp.dot(a, b, preferred_element_type=F32)


def _dot_f32(a, b):
    return jnp.dot(a, b, preferred_element_type=F32, precision=HIGHEST)


def _rmsnorm_kernel(x_ref, g_ref, o_ref):
    x = x_ref[...]
    ms = jnp.mean(x * x, axis=-1, keepdims=True)
    o_ref[...] = (x * lax.rsqrt(ms + EPS) * g_ref[...]).astype(o_ref.dtype)


def _rmsnorm(x, gain, out_dtype):
    m, d = x.shape
    tm = _pick(m, 512, SUBLANE)
    return pl.pallas_call(
        _rmsnorm_kernel,
        grid=(m // tm,),
        in_specs=[pl.BlockSpec((tm, d), lambda i: (i, 0)), pl.BlockSpec((1, d), lambda i: (0, 0))],
        out_specs=pl.BlockSpec((tm, d), lambda i: (i, 0)),
        out_shape=jax.ShapeDtypeStruct((m, d), out_dtype),
        compiler_params=_params(("parallel",), 4 * tm * d * 6),
        name="rmsnorm",
    )(x, gain.reshape(1, d))


def _mm_kernel(a_ref, w_ref, o_ref, *, gelu):
    y = _dot(a_ref[...], w_ref[...])
    if gelu:
        y = 0.5 * y * (1.0 + lax.erf(y * math.sqrt(0.5)))
    o_ref[...] = y.astype(o_ref.dtype)


def _mm(a, w, out_dtype, gelu=False, name="mm"):
    m, k = a.shape
    n = w.shape[1]
    osz = jnp.dtype(out_dtype).itemsize
    tm = _pick(m, 1024, SUBLANE)
    tn = _pick(n, 1024, LANE)
    est = lambda tn_: 2 * (tm * k * 2 + k * tn_ * 2 + tm * tn_ * osz) + tm * tn_ * 4
    while est(tn) > VMEM_BUDGET_BYTES and tn % (2 * LANE) == 0:
        tn //= 2
    return pl.pallas_call(
        functools.partial(_mm_kernel, gelu=gelu),
        grid=(m // tm, n // tn),
        in_specs=[pl.BlockSpec((tm, k), lambda i, j: (i, 0)), pl.BlockSpec((k, tn), lambda i, j: (0, j))],
        out_specs=pl.BlockSpec((tm, tn), lambda i, j: (i, j)),
        out_shape=jax.ShapeDtypeStruct((m, n), out_dtype),
        compiler_params=_params(("parallel", "parallel"), est(tn)),
        name=name,
    )(a, w)


def _norm_mm_kernel(a_ref, g_ref, w_ref, o_ref):
    x = a_ref[...]
    ms = jnp.mean(x * x, axis=-1, keepdims=True)
    xn = (x * lax.rsqrt(ms + EPS) * g_ref[...]).astype(BF16)
    o_ref[...] = _dot(xn, w_ref[...]).astype(o_ref.dtype)


def _norm_mm(a, gain, w, out_dtype, name="norm_mm"):
    m, k = a.shape
    n = w.shape[1]
    tm = _pick(m, 1024, SUBLANE)
    tn = _pick(n, 1024, LANE)
    est = 2 * (tm * k * 4 + k * tn * 2 + tm * tn * 4) + tm * k * 6 + tm * tn * 4
    return pl.pallas_call(
        _norm_mm_kernel,
        grid=(m // tm, n // tn),
        in_specs=[pl.BlockSpec((tm, k), lambda i, j: (i, 0)), pl.BlockSpec((1, k), lambda i, j: (0, 0)),
                  pl.BlockSpec((k, tn), lambda i, j: (0, j))],
        out_specs=pl.BlockSpec((tm, tn), lambda i, j: (i, j)),
        out_shape=jax.ShapeDtypeStruct((m, n), out_dtype),
        compiler_params=_params(("parallel", "parallel"), est),
        name=name,
    )(a, gain.reshape(1, k), w)


def _ffn_up_kernel(a_ref, wg_ref, wu_ref, o_ref):
    a = a_ref[...]
    g = _dot(a, wg_ref[...])
    u = _dot(a, wu_ref[...])
    o_ref[...] = (g * jax.nn.sigmoid(g) * u).astype(o_ref.dtype)


def _ffn_up(a, wg, wu):
    m, k = a.shape
    n = wg.shape[1]
    tm = _pick(m, 1024, SUBLANE)
    tn = _pick(n, 512, LANE)
    est = 2 * (tm * k * 2 + 2 * k * tn * 2 + tm * tn * 2) + 3 * tm * tn * 4
    return pl.pallas_call(
        _ffn_up_kernel,
        grid=(m // tm, n // tn),
        in_specs=[pl.BlockSpec((tm, k), lambda i, j: (i, 0)), pl.BlockSpec((k, tn), lambda i, j: (0, j)),
                  pl.BlockSpec((k, tn), lambda i, j: (0, j))],
        out_specs=pl.BlockSpec((tm, tn), lambda i, j: (i, j)),
        out_shape=jax.ShapeDtypeStruct((m, n), BF16),
        compiler_params=_params(("parallel", "parallel"), est),
        name="ffn_up",
    )(a, wg, wu)


def _mm_res_kernel(a_ref, w_ref, r_ref, o_ref, acc_ref, *, scale, nk):
    kk = pl.program_id(2)
    d = _dot(a_ref[...], w_ref[...])
    if nk == 1:
        o_ref[...] = r_ref[...] + scale * d
    else:
        @pl.when(kk == 0)
        def _():
            acc_ref[...] = d

        @pl.when(jnp.logical_and(kk > 0, kk < nk - 1))
        def _():
            acc_ref[...] += d

        @pl.when(kk == nk - 1)
        def _():
            o_ref[...] = r_ref[...] + scale * (acc_ref[...] + d)


def _mm_res(a, w, res, scale, name="mm_res"):
    m, k = a.shape
    n = w.shape[1]
    tm = _pick(m, 1024, SUBLANE)
    tk = _pick(k, 6144, LANE)
    tn = _pick(n, 1024, LANE)
    est = lambda tn_: 2 * (tm * tk * 2 + tk * tn_ * 2 + 2 * tm * tn_ * 4) + 2 * tm * tn_ * 4
    while est(tn) > VMEM_BUDGET_BYTES and tn % (2 * LANE) == 0:
        tn //= 2
    nk = k // tk
    return pl.pallas_call(
        functools.partial(_mm_res_kernel, scale=scale, nk=nk),
        grid=(m // tm, n // tn, nk),
        in_specs=[pl.BlockSpec((tm, tk), lambda i, j, kk: (i, kk)), pl.BlockSpec((tk, tn), lambda i, j, kk: (kk, j)),
                  pl.BlockSpec((tm, tn), lambda i, j, kk: (i, j))],
        out_specs=pl.BlockSpec((tm, tn), lambda i, j, kk: (i, j)),
        out_shape=jax.ShapeDtypeStruct((m, n), F32),
        scratch_shapes=[pltpu.VMEM((tm, tn), F32)],
        compiler_params=_params(("parallel", "parallel", "arbitrary"), est(tn)),
        name=name,
    )(a, w, res)


def _merge_kernel(h_ref, y_ref, wg_ref, wb_ref, o_ref, acc_ref):
    br = pl.program_id(2)
    gate = jax.nn.sigmoid(_dot(h_ref[...], wg_ref[...]))
    term = gate * _dot(y_ref[0], wb_ref[0])

    @pl.when(br == 0)
    def _():
        acc_ref[...] = term

    @pl.when(br == 1)
    def _():
        acc_ref[...] += term

    @pl.when(br == N_BRANCH - 1)
    def _():
        o_ref[...] = (acc_ref[...] + term).astype(o_ref.dtype)


def _merge(h, ys, w_gate, w_branch):
    m, d = h.shape
    wdt = ys.shape[2]
    tm = _pick(m, 1024, SUBLANE)
    tn = _pick(d, 512, LANE)
    nj = d // tn
    est = 2 * (tm * d * 2 + tm * wdt * 2 + d * tn * 2 + wdt * tn * 2 + tm * tn * 2) + 4 * tm * tn * 4
    return pl.pallas_call(
        _merge_kernel,
        grid=(m // tm, nj, N_BRANCH),
        in_specs=[pl.BlockSpec((tm, d), lambda i, j, b: (i, 0)),
                  pl.BlockSpec((1, tm, wdt), lambda i, j, b: (b, i, 0)),
                  pl.BlockSpec((d, tn), lambda i, j, b: (0, b * nj + j)),
                  pl.BlockSpec((1, wdt, tn), lambda i, j, b: (b, 0, j))],
        out_specs=pl.BlockSpec((tm, tn), lambda i, j, b: (i, j)),
        out_shape=jax.ShapeDtypeStruct((m, d), BF16),
        scratch_shapes=[pltpu.VMEM((tm, tn), F32)],
        compiler_params=_params(("parallel", "parallel", "arbitrary"), est),
        name="merge",
    )(h, ys, w_gate, w_branch)


def _gmlp_kernel(u_ref, v_ref, g_ref, ws_ref, bt_ref, o_ref):
    v = v_ref[...]
    ms = jnp.mean(v * v, axis=-1, keepdims=True)
    vn = (v * lax.rsqrt(ms + EPS) * g_ref[...]).astype(BF16)
    rows = v.shape[0]
    for c in range(rows // A_CHUNK):
        r0 = c * A_CHUNK
        for g in range(A_GROUPS):
            c0 = g * A_GROUP_DIM
            s = _dot(ws_ref[g], vn[r0:r0 + A_CHUNK, c0:c0 + A_GROUP_DIM]) + bt_ref[:, g:g + 1]
            o_ref[r0:r0 + A_CHUNK, c0:c0 + A_GROUP_DIM] = (
                u_ref[r0:r0 + A_CHUNK, c0:c0 + A_GROUP_DIM] * s).astype(o_ref.dtype)


def _gmlp(guv, v_gain, w_s, b_s):
    m = guv.shape[0]
    tm = _pick(m, 256, A_CHUNK)
    est = 2 * (2 * tm * A_WIDTH * 4 + tm * A_WIDTH * 2) + 3 * tm * A_WIDTH * 4
    return pl.pallas_call(
        _gmlp_kernel,
        grid=(m // tm,),
        in_specs=[pl.BlockSpec((tm, A_WIDTH), lambda i: (i, 0)), pl.BlockSpec((tm, A_WIDTH), lambda i: (i, 1)),
                  pl.BlockSpec((1, A_WIDTH), lambda i: (0, 0)),
                  pl.BlockSpec((A_GROUPS, A_CHUNK, A_CHUNK), lambda i: (0, 0, 0)),
                  pl.BlockSpec((A_CHUNK, A_GROUPS), lambda i: (0, 0))],
        out_specs=pl.BlockSpec((tm, A_WIDTH), lambda i: (i, 0)),
        out_shape=jax.ShapeDtypeStruct((m, A_WIDTH), BF16),
        compiler_params=_params(("parallel",), est),
        name="gmlp",
    )(guv, guv, v_gain.reshape(1, A_WIDTH), w_s.astype(BF16), b_s.T)


def _gdn_gate_kernel(s_ref, alog_ref, dtb_ref, gc_ref, beta_ref):
    x = s_ref[...]
    a = x[:, C_ROPE:C_ROPE + 2 * B_HEADS]
    b = x[:, C_ROPE + 2 * B_HEADS:C_ROPE + 4 * B_HEADS]
    z = a + dtb_ref[...]
    softplus = jnp.maximum(z, 0.0) + jnp.log1p(jnp.exp(-jnp.abs(z)))
    g = -jnp.exp(alog_ref[...]) * softplus
    beta_ref[...] = jax.nn.sigmoid(b)
    ri = lax.broadcasted_iota(jnp.int32, (B_CHUNK, B_CHUNK), 0)
    ci = lax.broadcasted_iota(jnp.int32, (B_CHUNK, B_CHUNK), 1)
    tril = (ri >= ci).astype(F32)
    triu = (ri <= ci).astype(F32)
    fwd = lax.broadcasted_iota(jnp.int32, (B_CHUNK, 2 * B_HEADS), 1) < B_HEADS
    for c in range(x.shape[0] // B_CHUNK):
        gcnk = g[c * B_CHUNK:(c + 1) * B_CHUNK]
        gc_ref[c * B_CHUNK:(c + 1) * B_CHUNK, :] = jnp.where(fwd, _dot_f32(tril, gcnk), _dot_f32(triu, gcnk))


def _gdn_gate(small, a_log, dt_bias):
    m = small.shape[0]
    tm = _pick(m, 512, B_CHUNK)
    nh = 2 * B_HEADS
    return pl.pallas_call(
        _gdn_gate_kernel,
        grid=(m // tm,),
        in_specs=[pl.BlockSpec((tm, LANE), lambda i: (i, 0)), pl.BlockSpec((1, nh), lambda i: (0, 0)),
                  pl.BlockSpec((1, nh), lambda i: (0, 0))],
        out_specs=[pl.BlockSpec((tm, nh), lambda i: (i, 0)), pl.BlockSpec((tm, nh), lambda i: (i, 0))],
        out_shape=[jax.ShapeDtypeStruct((m, nh), F32), jax.ShapeDtypeStruct((m, nh), F32)],
        compiler_params=_params(("parallel",), 1 << 20),
        name="gdn_gate",
    )(small, a_log.reshape(1, nh), dt_bias.reshape(1, nh))


def _conv_kernel(x_ref, p_ref, n_ref, w_ref, o_ref, xe_ref, *, ts, nblk):
    i = pl.program_id(1)
    j = pl.program_id(2)
    halo = SUBLANE
    xe_ref[0:halo, :] = jnp.where(i > 0, p_ref[0], 0.0)
    xe_ref[halo:halo + ts, :] = x_ref[0]
    xe_ref[halo + ts:halo + ts + halo, :] = jnp.where(i < nblk - 1, n_ref[0], 0.0)
    pad = (B_CONV - 1) // 2
    acc = None
    for t in range(B_CONV):
        term = w_ref[t:t + 1, :] * xe_ref[pl.ds(halo - pad + t, ts), :]
        acc = term if acc is None else acc + term
    y = acc * jax.nn.sigmoid(acc)
    qscale = jnp.where(j == 0, B_HEAD_DIM ** -0.5, 1.0).astype(F32)
    for h in range(B_HEADS):
        c0 = h * B_HEAD_DIM
        yh = y[:, c0:c0 + B_HEAD_DIM]
        ss = jnp.sum(yh * yh, axis=-1, keepdims=True)
        nrm = jnp.where(j < 2, lax.rsqrt(ss + EPS) * qscale, 1.0)
        o_ref[0, :, c0:c0 + B_HEAD_DIM] = yh * nrm


def _conv_qkv(qkvz, conv_w, batch, seq):
    x = qkvz.reshape(batch, seq, 4 * B_WIDTH)
    ts = _pick(seq, 256, SUBLANE)
    nblk = seq // ts
    hb = ts // SUBLANE
    nh8 = seq // SUBLANE
    tc = B_WIDTH
    est = 2 * (2 * ts * tc * 4 + 2 * SUBLANE * tc * 4) + 4 * (ts + 16) * tc * 4
    return pl.pallas_call(
        functools.partial(_conv_kernel, ts=ts, nblk=nblk),
        grid=(batch, nblk, 3),
        in_specs=[pl.BlockSpec((1, ts, tc), lambda b, i, j: (b, i, j)),
                  pl.BlockSpec((1, SUBLANE, tc), lambda b, i, j: (b, jnp.maximum(i * hb - 1, 0), j)),
                  pl.BlockSpec((1, SUBLANE, tc), lambda b, i, j: (b, jnp.minimum((i + 1) * hb, nh8 - 1), j)),
                  pl.BlockSpec((B_CONV, tc), lambda b, i, j: (0, j))],
        out_specs=pl.BlockSpec((1, ts, tc), lambda b, i, j: (b, i, j)),
        out_shape=jax.ShapeDtypeStruct((batch, seq, 3 * B_WIDTH), F32),
        scratch_shapes=[pltpu.VMEM((ts + 2 * SUBLANE, tc), F32)],
        compiler_params=_params(("parallel", "parallel", "parallel"), est),
        name="gdn_conv",
    )(x, x, x, conv_w)


def _unit_tri_inverse(a):
    c = a.shape[0]
    eye = (lax.broadcasted_iota(jnp.int32, (c, c), 0) == lax.broadcasted_iota(jnp.int32, (c, c), 1)).astype(F32)
    n = -a
    p = eye + n
    steps = int(math.log2(c)) - 1
    for _ in range(steps):
        n = _dot_f32(n, n)
        p = p + _dot_f32(p, n)
    return p


def _gdn_chunk(q, k, v, gcol, bcol, s_ref, incl, strict, last_row):
    c, dk = q.shape
    eg = jnp.exp(gcol)
    gl = gcol[last_row:last_row + 1, :]
    kb = k * bcol
    grow = jnp.transpose(jnp.broadcast_to(gcol, (c, dk)))[0:c, 0:c]
    decay = jnp.where(incl, jnp.exp(jnp.where(incl, gcol - grow, 0.0)), 0.0)
    kt = jnp.transpose(k).astype(BF16)
    x = _dot(jnp.concatenate([kb, q], axis=0).astype(BF16), kt)
    kk = x[0:c] * decay
    attn = x[c:2 * c] * decay
    tinv = _unit_tri_inverse(jnp.where(strict, kk, 0.0))
    rhs = jnp.concatenate([v * bcol, kb * eg], axis=1)
    uw = _dot_f32(tinv, rhs)
    u = uw[:, 0:dk]
    w = uw[:, dk:2 * dk]
    q_dec = q * eg
    k_dec = k * jnp.exp(gl - gcol)
    state = s_ref[...]
    ys = _dot(jnp.concatenate([w, q_dec], axis=0).astype(BF16), state.astype(BF16))
    v_new = u - ys[0:c]
    vb = v_new.astype(BF16)
    o = ys[c:2 * c] + _dot(attn.astype(BF16), vb)
    kv = lax.dot_general(k_dec.astype(BF16), vb, (((0,), (0,)), ((), ())), preferred_element_type=F32)
    s_ref[...] = state * jnp.exp(gl) + kv
    return o


def _gdn_kernel(qf_ref, kf_ref, vf_ref, gf_ref, bf_ref, qb_ref, kb_ref, vb_ref, gb_ref, bb_ref,
                of_ref, ob_ref, sf_ref, sb_ref, *, nchunk):
    h = pl.program_id(1)
    i = pl.program_id(2)

    @pl.when(i == 0)
    def _():
        sf_ref[...] = jnp.zeros_like(sf_ref)
        sb_ref[...] = jnp.zeros_like(sb_ref)

    ri = lax.broadcasted_iota(jnp.int32, (B_CHUNK, B_CHUNK), 0)
    ci = lax.broadcasted_iota(jnp.int32, (B_CHUNK, B_CHUNK), 1)
    lane = lax.broadcasted_iota(jnp.int32, (1, 2 * B_HEADS), 1)
    sel_f = lane == h
    sel_b = lane == h + B_HEADS

    def column(ref, r0, sel):
        return jnp.sum(jnp.where(sel, ref[0, r0:r0 + B_CHUNK, :], 0.0), axis=-1, keepdims=True)

    for c in range(nchunk):
        r0 = c * B_CHUNK
        of_ref[0, r0:r0 + B_CHUNK, :] = _gdn_chunk(
            qf_ref[0, r0:r0 + B_CHUNK, :], kf_ref[0, r0:r0 + B_CHUNK, :], vf_ref[0, r0:r0 + B_CHUNK, :],
            column(gf_ref, r0, sel_f), column(bf_ref, r0, sel_f), sf_ref, ri >= ci, ri > ci, B_CHUNK - 1)
        rb = (nchunk - 1 - c) * B_CHUNK
        ob_ref[0, rb:rb + B_CHUNK, :] = _gdn_chunk(
            qb_ref[0, rb:rb + B_CHUNK, :], kb_ref[0, rb:rb + B_CHUNK, :], vb_ref[0, rb:rb + B_CHUNK, :],
            column(gb_ref, rb, sel_b), column(bb_ref, rb, sel_b), sb_ref, ri <= ci, ri < ci, 0)


def _gdn(qkvn, gc, beta):
    batch, seq, _ = qkvn.shape
    rows = _pick(seq, 256, B_CHUNK)
    nb = seq // rows
    nh = 2 * B_HEADS
    hd = B_HEAD_DIM

    def spec(col0, rev):
        if rev:
            return pl.BlockSpec((1, rows, hd), lambda b, h, i: (b, nb - 1 - i, col0 + h))
        return pl.BlockSpec((1, rows, hd), lambda b, h, i: (b, i, col0 + h))

    def gspec(rev):
        if rev:
            return pl.BlockSpec((1, rows, nh), lambda b, h, i: (b, nb - 1 - i, 0))
        return pl.BlockSpec((1, rows, nh), lambda b, h, i: (b, i, 0))

    in_specs = []
    for rev in (False, True):
        in_specs += [spec(0, rev), spec(B_HEADS, rev), spec(2 * B_HEADS, rev), gspec(rev), gspec(rev)]
    out_sd = jax.ShapeDtypeStruct((batch, seq, B_WIDTH), F32)
    return pl.pallas_call(
        functools.partial(_gdn_kernel, nchunk=rows // B_CHUNK),
        grid=(batch, B_HEADS, nb),
        in_specs=in_specs,
        out_specs=[spec(0, False), spec(0, True)],
        out_shape=[out_sd, out_sd],
        scratch_shapes=[pltpu.VMEM((hd, hd), F32), pltpu.VMEM((hd, hd), F32)],
        compiler_params=_params(("parallel", "parallel", "arbitrary"), 8 << 20),
        name="gdn_scan",
    )(qkvn, qkvn, qkvn, gc, beta, qkvn, qkvn, qkvn, gc, beta)


def _gdn_out_kernel(of_ref, ob_ref, z_ref, g_ref, o_ref):
    for h in range(B_HEADS):
        c0 = h * B_HEAD_DIM
        o = of_ref[:, c0:c0 + B_HEAD_DIM] + ob_ref[:, c0:c0 + B_HEAD_DIM]
        ms = jnp.mean(o * o, axis=-1, keepdims=True)
        z = z_ref[:, c0:c0 + B_HEAD_DIM]
        o_ref[:, c0:c0 + B_HEAD_DIM] = (o * lax.rsqrt(ms + EPS) * g_ref[...] * (z * jax.nn.sigmoid(z))).astype(o_ref.dtype)


def _gdn_out(o_f, o_b, qkvz, o_gain):
    m = o_f.shape[0]
    tm = _pick(m, 512, SUBLANE)
    blk = lambda j: pl.BlockSpec((tm, B_WIDTH), lambda i: (i, j))
    return pl.pallas_call(
        _gdn_out_kernel,
        grid=(m // tm,),
        in_specs=[blk(0), blk(0), blk(3), pl.BlockSpec((1, B_HEAD_DIM), lambda i: (0, 0))],
        out_specs=blk(0),
        out_shape=jax.ShapeDtypeStruct((m, B_WIDTH), BF16),
        compiler_params=_params(("parallel",), 2 * tm * B_WIDTH * 14 + 4 * tm * B_WIDTH * 4),
        name="gdn_out",
    )(o_f, o_b, qkvz, o_gain.reshape(1, B_HEAD_DIM))


def _rope_tables_kernel(pos_ref, f_ref, cos_ref, sin_ref):
    ang = pos_ref[...].astype(F32) * f_ref[...]
    cos_ref[...] = jnp.cos(ang)
    sin_ref[...] = jnp.sin(ang)


def _rope_tables(positions):
    m = positions.size
    inv_freq = ROPE_THETA ** (-jnp.arange(0, C_ROPE, 2, dtype=F32) / C_ROPE)
    f = jnp.tile(inv_freq, LANE // (C_ROPE // 2)).reshape(1, LANE)
    tm = _pick(m, 1024, SUBLANE)
    sd = jax.ShapeDtypeStruct((m, LANE), F32)
    return pl.pallas_call(
        _rope_tables_kernel,
        grid=(m // tm,),
        in_specs=[pl.BlockSpec((tm, 1), lambda i: (i, 0)), pl.BlockSpec((1, LANE), lambda i: (0, 0))],
        out_specs=[pl.BlockSpec((tm, LANE), lambda i: (i, 0))] * 2,
        out_shape=[sd, sd],
        compiler_params=_params(("parallel",), 4 << 20),
        name="rope_tables",
    )(positions.reshape(m, 1), f)


def _rope_slab(t, cos, sin):
    half = C_ROPE // 2
    lane = lax.broadcasted_iota(jnp.int32, t.shape, 1)
    lo = lane < half
    mid = jnp.logical_and(lane >= half, lane < C_ROPE)
    t2_to_lo = pltpu.roll(t, LANE - half, 1)
    t1_to_mid = pltpu.roll(t, half, 1)
    return jnp.where(lo, t * cos - t2_to_lo * sin, jnp.where(mid, t1_to_mid * sin + t * cos, 0.0))


def _q_prep_kernel(q_ref, cos_ref, sin_ref, o_ref, *, scale):
    q = q_ref[0]
    qn = q[:, 0:C_NOPE]
    qr = _rope_slab(q[:, C_NOPE:C_QK_PAD], cos_ref[0], sin_ref[0])
    qc = jnp.concatenate([qn, qr], axis=1) * scale
    o_ref[0, 0] = jnp.transpose(qc).astype(o_ref.dtype)


def _q_prep(q, cos, sin, batch, seq):
    tm = _pick(seq, 512, LANE)
    scale = (C_NOPE + C_ROPE) ** -0.5
    return pl.pallas_call(
        functools.partial(_q_prep_kernel, scale=scale),
        grid=(batch, seq // tm, C_HEADS),
        in_specs=[pl.BlockSpec((1, tm, C_QK_PAD), lambda b, i, h: (b, i, h)),
                  pl.BlockSpec((1, tm, LANE), lambda b, i, h: (b, i, 0)),
                  pl.BlockSpec((1, tm, LANE), lambda b, i, h: (b, i, 0))],
        out_specs=pl.BlockSpec((1, 1, C_QK_PAD, tm), lambda b, i, h: (b, h, 0, i)),
        out_shape=jax.ShapeDtypeStruct((batch, C_HEADS, C_QK_PAD, seq), BF16),
        compiler_params=_params(("parallel", "parallel", "parallel"), 8 << 20),
        name="mla_q_prep",
    )(q.reshape(batch, seq, C_HEADS * C_QK_PAD), cos.reshape(batch, seq, LANE), sin.reshape(batch, seq, LANE))


def _kv_prep_kernel(kv_ref, kr_ref, cos_ref, sin_ref, k_ref, vt_ref):
    kv = kv_ref[0]
    lane = lax.broadcasted_iota(jnp.int32, (kv.shape[0], LANE), 1)
    kr = _rope_slab(jnp.where(lane < C_ROPE, kr_ref[0], 0.0), cos_ref[0], sin_ref[0])
    k_ref[0, 0] = jnp.concatenate([kv[:, 0:C_NOPE], kr], axis=1).astype(k_ref.dtype)
    vt_ref[0, 0] = jnp.transpose(kv[:, C_NOPE:C_NOPE + C_V]).astype(vt_ref.dtype)


def _kv_prep(kv, small, cos, sin, batch, seq):
    tm = _pick(seq, 512, LANE)
    r3 = lambda t, w: t.reshape(batch, seq, w)
    return pl.pallas_call(
        _kv_prep_kernel,
        grid=(batch, seq // tm, C_HEADS),
        in_specs=[pl.BlockSpec((1, tm, C_NOPE + C_V), lambda b, i, h: (b, i, h)),
                  pl.BlockSpec((1, tm, LANE), lambda b, i, h: (b, i, 0)),
                  pl.BlockSpec((1, tm, LANE), lambda b, i, h: (b, i, 0)),
                  pl.BlockSpec((1, tm, LANE), lambda b, i, h: (b, i, 0))],
        out_specs=[pl.BlockSpec((1, 1, tm, C_QK_PAD), lambda b, i, h: (b, h, i, 0)),
                   pl.BlockSpec((1, 1, C_V, tm), lambda b, i, h: (b, h, 0, i))],
        out_shape=[jax.ShapeDtypeStruct((batch, C_HEADS, seq, C_QK_PAD), BF16),
                   jax.ShapeDtypeStruct((batch, C_HEADS, C_V, seq), BF16)],
        compiler_params=_params(("parallel", "parallel", "parallel"), 8 << 20),
        name="mla_kv_prep",
    )(r3(kv, C_HEADS * (C_NOPE + C_V)), r3(small, LANE), r3(cos, LANE), r3(sin, LANE))


def _attn_kernel(qt_ref, k_ref, vt_ref, o_ref, *, tkv):
    qt = qt_ref[0, 0]
    tq = qt.shape[1]
    nkv = k_ref.shape[2] // tkv

    def body(j, carry):
        m, l, acc = carry
        k0 = pl.multiple_of(j * tkv, tkv)
        s = _dot(k_ref[0, 0, pl.ds(k0, tkv), :], qt)
        m_new = jnp.maximum(m, jnp.max(s, axis=0, keepdims=True))
        alpha = jnp.exp(m - m_new)
        p = jnp.exp(s - m_new)
        l = alpha * l + jnp.sum(p, axis=0, keepdims=True)
        acc = alpha * acc + _dot(vt_ref[0, 0, :, pl.ds(k0, tkv)], p.astype(BF16))
        return m_new, l, acc

    init = (jnp.full((1, tq), -jnp.inf, F32), jnp.zeros((1, tq), F32), jnp.zeros((C_V, tq), F32))
    m, l, acc = lax.fori_loop(0, nkv, body, init)
    o_ref[0] = jnp.transpose(acc / l).astype(o_ref.dtype)


def _attention(qt, kc, vt):
    batch, heads, _, seq = qt.shape
    tq = _pick(seq, 1024, LANE)
    tkv = _pick(seq, 512, LANE)
    est = 2 * (C_QK_PAD * tq * 2 + seq * C_QK_PAD * 2 + C_V * seq * 2 + tq * C_V * 2) + 4 * tkv * tq * 4
    return pl.pallas_call(
        functools.partial(_attn_kernel, tkv=tkv),
        grid=(batch, heads, seq // tq),
        in_specs=[pl.BlockSpec((1, 1, C_QK_PAD, tq), lambda b, h, i: (b, h, 0, i)),
                  pl.BlockSpec((1, 1, seq, C_QK_PAD), lambda b, h, i: (b, h, 0, 0)),
                  pl.BlockSpec((1, 1, C_V, seq), lambda b, h, i: (b, h, 0, 0))],
        out_specs=pl.BlockSpec((1, tq, C_V), lambda b, h, i: (b, i, h)),
        out_shape=jax.ShapeDtypeStruct((batch, seq, heads * C_V), BF16),
        compiler_params=_params(("parallel", "parallel", "parallel"), est),
        name="mla_attention",
    )(qt, kc, vt)


def _pad_cols(w, n):
    return jnp.pad(w, ((0, 0), (0, n - w.shape[1])))


def _layer_weights(p, l, d_ff, d_ffp):
    bf = lambda t: t.astype(BF16)
    o = {}
    for tag in ("ffn1", "ffn2"):
        up = p[tag + "_up"][l]
        o[tag + "_wg"] = _pad_cols(bf(up[:, :d_ff]), d_ffp)
        o[tag + "_wu"] = _pad_cols(bf(up[:, d_ff:]), d_ffp)
        o[tag + "_wd"] = jnp.pad(bf(p[tag + "_down"][l]), ((0, d_ffp - d_ff), (0, 0)))
    w_in = p["w_in"][l]
    c0 = 0
    seg = {}
    for name, width in (("a", 2 * A_WIDTH), ("qkvz", 4 * B_WIDTH), ("ab", 4 * B_HEADS), ("dq", C_Q_LORA),
                        ("dkv", C_KV_LORA), ("kr", C_ROPE)):
        seg[name] = w_in[:, c0:c0 + width]
        c0 += width
    o["w_a"] = bf(seg["a"])
    o["w_qkvz"] = bf(seg["qkvz"])
    o["w_dq"] = bf(seg["dq"])
    o["w_dkv_small"] = bf(jnp.concatenate([seg["dkv"], seg["kr"], seg["ab"]], axis=1))
    uq = p["c_w_uq"][l].reshape(C_Q_LORA, C_HEADS, C_NOPE + C_ROPE)
    o["w_uq"] = bf(jnp.pad(uq, ((0, 0), (0, 0), (0, C_QK_PAD - C_NOPE - C_ROPE))).reshape(C_Q_LORA, C_HEADS * C_QK_PAD))
    o["w_ukv"] = bf(p["c_w_ukv"][l])
    o["w_gate"] = bf(p["w_gate"][l])
    o["w_branch"] = bf(p["w_branch"][l])
    o["w_o"] = bf(p["w_o"][l])
    return o


def kernel(x, positions, norm_ffn1, ffn1_up, ffn1_down, norm_mix, w_in, w_gate, a_v_gain, a_w_s, a_b_s, b_conv,
           b_a_log, b_dt_bias, b_o_gain, c_q_gain, c_kv_gain, c_w_uq, c_w_ukv, w_branch, w_o, norm_ffn2, ffn2_up,
           ffn2_down, norm_final):
    batch, seq, d = x.shape
    depth = norm_ffn1.shape[0]
    d_ff = ffn1_down.shape[1]
    d_ffp = -(-d_ff // 1024) * 1024 if d_ff > 1024 else -(-d_ff // LANE) * LANE
    m = batch * seq
    p = dict(ffn1_up=ffn1_up, ffn1_down=ffn1_down, ffn2_up=ffn2_up, ffn2_down=ffn2_down, w_in=w_in, w_gate=w_gate,
             c_w_uq=c_w_uq, c_w_ukv=c_w_ukv, w_branch=w_branch, w_o=w_o)

    cos, sin = _rope_tables(positions)
    xs = x.reshape(m, d)
    for l in range(depth):
        w = _layer_weights(p, l, d_ff, d_ffp)

        hn = _rmsnorm(xs, norm_ffn1[l], BF16)
        xs = _mm_res(_ffn_up(hn, w["ffn1_wg"], w["ffn1_wu"]), w["ffn1_wd"], xs, 0.5, name="ffn_down")

        h = _rmsnorm(xs, norm_mix[l], BF16)
        guv = _mm(h, w["w_a"], F32, gelu=True, name="proj_a")
        qkvz = _mm(h, w["w_qkvz"], F32, name="proj_b")
        dq = _mm(h, w["w_dq"], F32, name="proj_dq")
        dkv_small = _mm(h, w["w_dkv_small"], F32, name="proj_dkv")
        dkv = dkv_small[:, :C_KV_LORA]
        small = dkv_small[:, C_KV_LORA:]

        y_a = _gmlp(guv, a_v_gain[l], a_w_s[l], a_b_s[l])

        gc, beta = _gdn_gate(small, b_a_log[l], b_dt_bias[l])
        qkvn = _conv_qkv(qkvz, b_conv[l], batch, seq)
        nh = 2 * B_HEADS
        o_f, o_b = _gdn(qkvn, gc.reshape(batch, seq, nh), beta.reshape(batch, seq, nh))
        y_b = _gdn_out(o_f.reshape(m, B_WIDTH), o_b.reshape(m, B_WIDTH), qkvz, b_o_gain[l])

        q = _norm_mm(dq, c_q_gain[l], w["w_uq"], F32, name="mla_uq")
        kv = _norm_mm(dkv, c_kv_gain[l], w["w_ukv"], F32, name="mla_ukv")
        qt = _q_prep(q, cos, sin, batch, seq)
        kc, vt = _kv_prep(kv, small, cos, sin, batch, seq)
        y_c = _attention(qt, kc, vt).reshape(m, C_WIDTH)

        merged = _merge(h, jnp.stack([y_a, y_b, y_c], axis=0), w["w_gate"], w["w_branch"])
        xs = _mm_res(merged, w["w_o"], xs, 1.0, name="out_proj")

        hn = _rmsnorm(xs, norm_ffn2[l], BF16)
        xs = _mm_res(_ffn_up(hn, w["ffn2_wg"], w["ffn2_wu"]), w["ffn2_wd"], xs, 0.5, name="ffn_down")

    return _rmsnorm(xs, norm_final, F32).reshape(batch, seq, d)
```

```python
import functools
import math

import jax
import jax.numpy as jnp
from jax import lax
from jax.experimental import pallas as pl
from jax.experimental.pallas import tpu as pltpu

F32 = jnp.float32
BF16 = jnp.bfloat16
EPS = 1e-6
ROPE_THETA = 10000.0

A_GROUPS, A_GROUP_DIM, A_CHUNK = 16, 128, 128
A_WIDTH = A_GROUPS * A_GROUP_DIM
B_HEADS, B_HEAD_DIM, B_CONV, B_CHUNK = 16, 128, 5, 64
B_WIDTH = B_HEADS * B_HEAD_DIM
B_GROUP = 256
C_HEADS, C_Q_LORA, C_KV_LORA, C_NOPE, C_ROPE, C_V = 16, 1024, 512, 128, 64, 128
C_WIDTH = C_HEADS * C_V
C_QK_PAD = 256
N_BRANCH = 3

VMEM_BUDGET_BYTES = 44 * 1024 * 1024
VMEM_LIMIT_CAP_BYTES = 56 * 1024 * 1024
LANE = 128
SUBLANE = 8
HIGHEST = lax.Precision.HIGHEST


def _pick(n, pref, align):
    if n <= pref:
        return n
    t = (pref // align) * align
    while t >= align:
        if n % t == 0:
            return t
        t -= align
    return n


def _params(sem, vmem_bytes):
    limit = int(min(VMEM_LIMIT_CAP_BYTES, max(vmem_bytes + (8 << 20), 32 << 20)))
    return pltpu.CompilerParams(dimension_semantics=sem, vmem_limit_bytes=limit)


def _dot(a, b):
    return jnp.dot(a, b, preferred_element_type=F32)


def _dot_f32(a, b):
    return jnp.dot(a, b, preferred_element_type=F32, precision=HIGHEST)


def _rmsnorm_kernel(x_ref, g_ref, o_ref):
    x = x_ref[...]
    ms = jnp.mean(x * x, axis=-1, keepdims=True)
    o_ref[...] = (x * lax.rsqrt(ms + EPS) * g_ref[...]).astype(o_ref.dtype)


def _rmsnorm(x, gain, out_dtype):
    m, d = x.shape
    tm = _pick(m, 512, SUBLANE)
    return pl.pallas_call(
        _rmsnorm_kernel,
        grid=(m // tm,),
        in_specs=[pl.BlockSpec((tm, d), lambda i: (i, 0)), pl.BlockSpec((1, d), lambda i: (0, 0))],
        out_specs=pl.BlockSpec((tm, d), lambda i: (i, 0)),
        out_shape=jax.ShapeDtypeStruct((m, d), out_dtype),
        compiler_params=_params(("parallel",), 4 * tm * d * 6),
        name="rmsnorm",
    )(x, gain.reshape(1, d))


def _mm_kernel(a_ref, w_ref, o_ref, *, gelu):
    y = _dot(a_ref[...], w_ref[...])
    if gelu:
        y = 0.5 * y * (1.0 + lax.erf(y * math.sqrt(0.5)))
    o_ref[...] = y.astype(o_ref.dtype)


def _mm(a, w, out_dtype, gelu=False, name="mm"):
    m, k = a.shape
    n = w.shape[1]
    osz = jnp.dtype(out_dtype).itemsize
    tm = _pick(m, 1024, SUBLANE)
    tn = _pick(n, 1024, LANE)
    est = lambda tn_: 2 * (tm * k * 2 + k * tn_ * 2 + tm * tn_ * osz) + tm * tn_ * 4
    while est(tn) > VMEM_BUDGET_BYTES and tn % (2 * LANE) == 0:
        tn //= 2
    return pl.pallas_call(
        functools.partial(_mm_kernel, gelu=gelu),
        grid=(m // tm, n // tn),
        in_specs=[pl.BlockSpec((tm, k), lambda i, j: (i, 0)), pl.BlockSpec((k, tn), lambda i, j: (0, j))],
        out_specs=pl.BlockSpec((tm, tn), lambda i, j: (i, j)),
        out_shape=jax.ShapeDtypeStruct((m, n), out_dtype),
        compiler_params=_params(("parallel", "parallel"), est(tn)),
        name=name,
    )(a, w)


def _norm_mm_kernel(a_ref, g_ref, w_ref, o_ref):
    x = a_ref[...]
    ms = jnp.mean(x * x, axis=-1, keepdims=True)
    xn = (x * lax.rsqrt(ms + EPS) * g_ref[...]).astype(BF16)
    o_ref[...] = _dot(xn, w_ref[...]).astype(o_ref.dtype)


def _norm_mm(a, gain, w, out_dtype, name="norm_mm"):
    m = a.shape[0]
    k, n = w.shape
    tm = _pick(m, 1024, SUBLANE)
    tn = _pick(n, 1024, LANE)
    est = 2 * (tm * k * 4 + k * tn * 2 + tm * tn * 4) + tm * k * 6 + tm * tn * 4
    return pl.pallas_call(
        _norm_mm_kernel,
        grid=(m // tm, n // tn),
        in_specs=[pl.BlockSpec((tm, k), lambda i, j: (i, 0)), pl.BlockSpec((1, k), lambda i, j: (0, 0)),
                  pl.BlockSpec((k, tn), lambda i, j: (0, j))],
        out_specs=pl.BlockSpec((tm, tn), lambda i, j: (i, j)),
        out_shape=jax.ShapeDtypeStruct((m, n), out_dtype),
        compiler_params=_params(("parallel", "parallel"), est),
        name=name,
    )(a, gain.reshape(1, k), w)


def _ffn_up_kernel(a_ref, wg_ref, wu_ref, o_ref):
    a = a_ref[...]
    g = _dot(a, wg_ref[...])
    u = _dot(a, wu_ref[...])
    o_ref[...] = (g * jax.nn.sigmoid(g) * u).astype(o_ref.dtype)


def _ffn_up(a, wg, wu):
    m, k = a.shape
    n = wg.shape[1]
    tm = _pick(m, 1024, SUBLANE)
    tn = _pick(n, 512, LANE)
    est = 2 * (tm * k * 2 + 2 * k * tn * 2 + tm * tn * 2) + 3 * tm * tn * 4
    return pl.pallas_call(
        _ffn_up_kernel,
        grid=(m // tm, n // tn),
        in_specs=[pl.BlockSpec((tm, k), lambda i, j: (i, 0)), pl.BlockSpec((k, tn), lambda i, j: (0, j)),
                  pl.BlockSpec((k, tn), lambda i, j: (0, j))],
        out_specs=pl.BlockSpec((tm, tn), lambda i, j: (i, j)),
        out_shape=jax.ShapeDtypeStruct((m, n), BF16),
        compiler_params=_params(("parallel", "parallel"), est),
        name="ffn_up",
    )(a, wg, wu)


def _mm_res_kernel(a_ref, w_ref, r_ref, o_ref, *, scale):
    o_ref[...] = r_ref[...] + scale * _dot(a_ref[...], w_ref[...])


def _mm_res(a, w, res, scale, name="mm_res"):
    m, k = a.shape
    n = w.shape[1]
    tm = _pick(m, 1024, SUBLANE)
    a_bufs = 2 if 2 * tm * k * 2 <= VMEM_BUDGET_BYTES // 2 else 1
    tn = _pick(n, 1024, LANE)
    est = lambda tn_: a_bufs * tm * k * 2 + 2 * (k * tn_ * 2 + 2 * tm * tn_ * 4) + tm * tn_ * 4
    while est(tn) > VMEM_BUDGET_BYTES and tn % (2 * LANE) == 0:
        tn //= 2
    a_spec = pl.BlockSpec((tm, k), lambda i, j: (i, 0), pipeline_mode=pl.Buffered(a_bufs))
    return pl.pallas_call(
        functools.partial(_mm_res_kernel, scale=scale),
        grid=(m // tm, n // tn),
        in_specs=[a_spec, pl.BlockSpec((k, tn), lambda i, j: (0, j)), pl.BlockSpec((tm, tn), lambda i, j: (i, j))],
        out_specs=pl.BlockSpec((tm, tn), lambda i, j: (i, j)),
        out_shape=jax.ShapeDtypeStruct((m, n), F32),
        compiler_params=_params(("parallel", "arbitrary"), est(tn)),
        name=name,
    )(a, w, res)


def _merge_kernel(h_ref, y_ref, wg_ref, wb_ref, o_ref, acc_ref):
    br = pl.program_id(2)
    gate = jax.nn.sigmoid(_dot(h_ref[...], wg_ref[...]))
    term = gate * _dot(y_ref[0], wb_ref[0])

    @pl.when(br == 0)
    def _():
        acc_ref[...] = term

    @pl.when(br == 1)
    def _():
        acc_ref[...] += term

    @pl.when(br == N_BRANCH - 1)
    def _():
        o_ref[...] = (acc_ref[...] + term).astype(o_ref.dtype)


def _merge(h, ys, w_gate, w_branch):
    m, d = h.shape
    wdt = ys.shape[2]
    tm = _pick(m, 1024, SUBLANE)
    tn = _pick(d, 512, LANE)
    nj = d // tn
    est = 2 * (tm * d * 2 + tm * wdt * 2 + d * tn * 2 + wdt * tn * 2 + tm * tn * 2) + 4 * tm * tn * 4
    return pl.pallas_call(
        _merge_kernel,
        grid=(m // tm, nj, N_BRANCH),
        in_specs=[pl.BlockSpec((tm, d), lambda i, j, b: (i, 0)),
                  pl.BlockSpec((1, tm, wdt), lambda i, j, b: (b, i, 0)),
                  pl.BlockSpec((d, tn), lambda i, j, b: (0, b * nj + j)),
                  pl.BlockSpec((1, wdt, tn), lambda i, j, b: (b, 0, j))],
        out_specs=pl.BlockSpec((tm, tn), lambda i, j, b: (i, j)),
        out_shape=jax.ShapeDtypeStruct((m, d), BF16),
        scratch_shapes=[pltpu.VMEM((tm, tn), F32)],
        compiler_params=_params(("parallel", "parallel", "arbitrary"), est),
        name="merge",
    )(h, ys, w_gate, w_branch)


def _gmlp_kernel(u_ref, v_ref, g_ref, ws_ref, bt_ref, o_ref):
    v = v_ref[...]
    ms = jnp.mean(v * v, axis=-1, keepdims=True)
    vn = (v * lax.rsqrt(ms + EPS) * g_ref[...]).astype(BF16)
    rows = v.shape[0]
    for c in range(rows // A_CHUNK):
        r0 = c * A_CHUNK
        for g in range(A_GROUPS):
            c0 = g * A_GROUP_DIM
            s = _dot(ws_ref[g], vn[r0:r0 + A_CHUNK, c0:c0 + A_GROUP_DIM]) + bt_ref[:, g:g + 1]
            o_ref[r0:r0 + A_CHUNK, c0:c0 + A_GROUP_DIM] = (
                u_ref[r0:r0 + A_CHUNK, c0:c0 + A_GROUP_DIM] * s).astype(o_ref.dtype)


def _gmlp(guv, v_gain, w_s, b_s):
    m = guv.shape[0]
    tm = _pick(m, 256, A_CHUNK)
    est = 2 * (2 * tm * A_WIDTH * 4 + tm * A_WIDTH * 2) + 3 * tm * A_WIDTH * 4
    return pl.pallas_call(
        _gmlp_kernel,
        grid=(m // tm,),
        in_specs=[pl.BlockSpec((tm, A_WIDTH), lambda i: (i, 0)), pl.BlockSpec((tm, A_WIDTH), lambda i: (i, 1)),
                  pl.BlockSpec((1, A_WIDTH), lambda i: (0, 0)),
                  pl.BlockSpec((A_GROUPS, A_CHUNK, A_CHUNK), lambda i: (0, 0, 0)),
                  pl.BlockSpec((A_CHUNK, A_GROUPS), lambda i: (0, 0))],
        out_specs=pl.BlockSpec((tm, A_WIDTH), lambda i: (i, 0)),
        out_shape=jax.ShapeDtypeStruct((m, A_WIDTH), BF16),
        compiler_params=_params(("parallel",), est),
        name="gmlp",
    )(guv, guv, v_gain.reshape(1, A_WIDTH), w_s.astype(BF16), b_s.T)


def _gdn_gate_kernel(s_ref, alog_ref, dtb_ref, gc_ref, beta_ref):
    x = s_ref[...]
    a = x[:, C_ROPE:C_ROPE + 2 * B_HEADS]
    b = x[:, C_ROPE + 2 * B_HEADS:C_ROPE + 4 * B_HEADS]
    z = a + dtb_ref[...]
    softplus = jnp.maximum(z, 0.0) + jnp.log1p(jnp.exp(-jnp.abs(z)))
    g = -jnp.exp(alog_ref[...]) * softplus
    beta_ref[...] = jax.nn.sigmoid(b)
    ri = lax.broadcasted_iota(jnp.int32, (B_CHUNK, B_CHUNK), 0)
    ci = lax.broadcasted_iota(jnp.int32, (B_CHUNK, B_CHUNK), 1)
    tril = (ri >= ci).astype(F32)
    triu = (ri <= ci).astype(F32)
    fwd = lax.broadcasted_iota(jnp.int32, (B_CHUNK, 2 * B_HEADS), 1) < B_HEADS
    for c in range(x.shape[0] // B_CHUNK):
        gcnk = g[c * B_CHUNK:(c + 1) * B_CHUNK]
        gc_ref[c * B_CHUNK:(c + 1) * B_CHUNK, :] = jnp.where(fwd, _dot_f32(tril, gcnk), _dot_f32(triu, gcnk))


def _gdn_gate(dkv_small, a_log, dt_bias):
    m = dkv_small.shape[0]
    tm = _pick(m, 512, B_CHUNK)
    nh = 2 * B_HEADS
    small_blk = C_KV_LORA // LANE
    return pl.pallas_call(
        _gdn_gate_kernel,
        grid=(m // tm,),
        in_specs=[pl.BlockSpec((tm, LANE), lambda i: (i, small_blk)), pl.BlockSpec((1, nh), lambda i: (0, 0)),
                  pl.BlockSpec((1, nh), lambda i: (0, 0))],
        out_specs=[pl.BlockSpec((tm, nh), lambda i: (i, 0)), pl.BlockSpec((tm, nh), lambda i: (i, 0))],
        out_shape=[jax.ShapeDtypeStruct((m, nh), F32), jax.ShapeDtypeStruct((m, nh), F32)],
        compiler_params=_params(("parallel",), 1 << 20),
        name="gdn_gate",
    )(dkv_small, a_log.reshape(1, nh), dt_bias.reshape(1, nh))


def _conv_kernel(x_ref, p_ref, n_ref, w_ref, o_ref, xe_ref, *, ts, nblk):
    i = pl.program_id(1)
    j = pl.program_id(2)
    halo = SUBLANE
    xe_ref[0:halo, :] = jnp.where(i > 0, p_ref[0], 0.0)
    xe_ref[halo:halo + ts, :] = x_ref[0]
    xe_ref[halo + ts:halo + ts + halo, :] = jnp.where(i < nblk - 1, n_ref[0], 0.0)
    pad = (B_CONV - 1) // 2
    acc = None
    for t in range(B_CONV):
        term = w_ref[t:t + 1, :] * xe_ref[pl.ds(halo - pad + t, ts), :]
        acc = term if acc is None else acc + term
    y = acc * jax.nn.sigmoid(acc)
    qscale = jnp.where(j == 0, B_HEAD_DIM ** -0.5, 1.0).astype(F32)
    for h in range(B_HEADS):
        c0 = h * B_HEAD_DIM
        yh = y[:, c0:c0 + B_HEAD_DIM]
        ss = jnp.sum(yh * yh, axis=-1, keepdims=True)
        nrm = jnp.where(j < 2, lax.rsqrt(ss + EPS) * qscale, 1.0)
        o_ref[0, :, c0:c0 + B_HEAD_DIM] = yh * nrm


def _conv_qkv(qkvz, conv_w, batch, seq):
    x = qkvz.reshape(batch, seq, 4 * B_WIDTH)
    ts = _pick(seq, 256, SUBLANE)
    nblk = seq // ts
    hb = ts // SUBLANE
    nh8 = seq // SUBLANE
    tc = B_WIDTH
    est = 2 * (2 * ts * tc * 4 + 2 * SUBLANE * tc * 4) + 4 * (ts + 16) * tc * 4
    return pl.pallas_call(
        functools.partial(_conv_kernel, ts=ts, nblk=nblk),
        grid=(batch, nblk, 3),
        in_specs=[pl.BlockSpec((1, ts, tc), lambda b, i, j: (b, i, j)),
                  pl.BlockSpec((1, SUBLANE, tc), lambda b, i, j: (b, jnp.maximum(i * hb - 1, 0), j)),
                  pl.BlockSpec((1, SUBLANE, tc), lambda b, i, j: (b, jnp.minimum((i + 1) * hb, nh8 - 1), j)),
                  pl.BlockSpec((B_CONV, tc), lambda b, i, j: (0, j))],
        out_specs=pl.BlockSpec((1, ts, tc), lambda b, i, j: (b, i, j)),
        out_shape=jax.ShapeDtypeStruct((batch, seq, 3 * B_WIDTH), F32),
        scratch_shapes=[pltpu.VMEM((ts + 2 * SUBLANE, tc), F32)],
        compiler_params=_params(("parallel", "parallel", "parallel"), est),
        name="gdn_conv",
    )(x, x, x, conv_w)


def _gdn_kernel(qf_ref, kf_ref, vf_ref, ktf_ref, gf_ref, bf_ref, gtf_ref,
                qb_ref, kb_ref, vb_ref, ktb_ref, gb_ref, bb_ref, gtb_ref,
                of_ref, ob_ref, sf_ref, sb_ref, *, ngroup):
    h = pl.program_id(1)
    i = pl.program_id(2)

    @pl.when(i == 0)
    def _():
        sf_ref[...] = jnp.zeros_like(sf_ref)
        sb_ref[...] = jnp.zeros_like(sb_ref)

    g_rows, c_rows, dk = B_GROUP, B_CHUNK, B_HEAD_DIM
    cpg = g_rows // c_rows
    rid = lax.broadcasted_iota(jnp.int32, (g_rows, g_rows), 0)
    cid = lax.broadcasted_iota(jnp.int32, (g_rows, g_rows), 1)
    same_chunk = (rid // c_rows) == (cid // c_rows)
    eye = (rid == cid).astype(F32)
    lane = lax.broadcasted_iota(jnp.int32, (1, 2 * B_HEADS), 1)
    dirs = (
        dict(q=qf_ref, k=kf_ref, v=vf_ref, kt=ktf_ref, g=gf_ref, b=bf_ref, gt=gtf_ref, o=of_ref, s=sf_ref,
             col=h, incl=jnp.logical_and(same_chunk, rid >= cid), strict=jnp.logical_and(same_chunk, rid > cid),
             last=c_rows - 1, rev=False),
        dict(q=qb_ref, k=kb_ref, v=vb_ref, kt=ktb_ref, g=gb_ref, b=bb_ref, gt=gtb_ref, o=ob_ref, s=sb_ref,
             col=h + B_HEADS, incl=jnp.logical_and(same_chunk, rid <= cid),
             strict=jnp.logical_and(same_chunk, rid < cid), last=0, rev=True),
    )

    units = []
    for d in dirs:
        sel = lane == d["col"]
        grow_all = d["gt"][0, pl.ds(d["col"], 1), :]
        order = range(ngroup - 1, -1, -1) if d["rev"] else range(ngroup)
        for gi in order:
            r0 = gi * g_rows
            q = d["q"][0, r0:r0 + g_rows, :]
            k = d["k"][0, r0:r0 + g_rows, :]
            v = d["v"][0, r0:r0 + g_rows, :]
            kt = d["kt"][0, 0, :, r0:r0 + g_rows]
            gcol = jnp.sum(jnp.where(sel, d["g"][0, r0:r0 + g_rows, :], 0.0), axis=-1, keepdims=True)
            bcol = jnp.sum(jnp.where(sel, d["b"][0, r0:r0 + g_rows, :], 0.0), axis=-1, keepdims=True)
            grow = grow_all[:, r0:r0 + g_rows]
            eg = jnp.exp(gcol)
            kbeta = k * bcol
            decay = jnp.where(d["incl"], jnp.exp(jnp.where(d["incl"], gcol - grow, 0.0)), 0.0)
            x = _dot(jnp.concatenate([kbeta, q], axis=0).astype(BF16), kt.astype(BF16))
            n = -jnp.where(d["strict"], x[0:g_rows] * decay, 0.0)
            units.append(dict(d=d, r0=r0, q=q, kt=kt, eg=eg, grow=grow, n=n, attn=x[g_rows:] * decay,
                              rhs=jnp.concatenate([v * bcol, kbeta * eg], axis=1)))

    for u in units:
        u["nb"] = u["n"].astype(BF16)
        u["p"] = eye + u["n"]
    for _ in range(int(math.log2(c_rows)) - 1):
        for u in units:
            u["nb"] = _dot(u["nb"], u["nb"]).astype(BF16)
        for u in units:
            u["p"] = u["p"] + _dot(u["p"].astype(BF16), u["nb"])

    for u in units:
        u["uw"] = _dot(u["p"].astype(BF16), u["rhs"].astype(BF16)).astype(BF16)
    for u in units:
        au = _dot(u["attn"].astype(BF16), u["uw"])
        u["ob"] = au[:, 0:dk]
        u["qeff"] = u["q"] * u["eg"] - au[:, dk:2 * dk]

    steps = {False: [], True: []}
    for u in units:
        d = u["d"]
        corder = range(cpg - 1, -1, -1) if d["rev"] else range(cpg)
        for c in corder:
            c0 = c * c_rows
            growc = u["grow"][:, c0:c0 + c_rows]
            glc = growc[:, d["last"]:d["last"] + 1]
            kdt = (u["kt"][:, c0:c0 + c_rows] * jnp.exp(glc - growc)).astype(BF16)
            kuw = _dot(kdt, u["uw"][c0:c0 + c_rows, :])
            lhs = jnp.concatenate([-kuw[:, dk:2 * dk], u["qeff"][c0:c0 + c_rows]], axis=0).astype(BF16)
            steps[d["rev"]].append(dict(lhs=lhs, bmat=kuw[:, 0:dk], egl=jnp.exp(glc),
                                        ob=u["ob"][c0:c0 + c_rows], row=u["r0"] + c0))

    state = {False: sf_ref[...], True: sb_ref[...]}
    for t in range(ngroup * cpg):
        for d in dirs:
            st = steps[d["rev"]][t]
            ys = _dot(st["lhs"], state[d["rev"]].astype(BF16))
            state[d["rev"]] = state[d["rev"]] * st["egl"] + ys[0:dk] + st["bmat"]
            d["o"][0, st["row"]:st["row"] + c_rows, :] = ys[dk:dk + c_rows] + st["ob"]
    sf_ref[...] = state[False]
    sb_ref[...] = state[True]


def _gdn(qkvn, gc, beta):
    batch, seq, _ = qkvn.shape
    rows = _pick(seq, 512, B_GROUP)
    nb = seq // rows
    nh = 2 * B_HEADS
    hd = B_HEAD_DIM
    k_t = jnp.transpose(qkvn[:, :, B_WIDTH:2 * B_WIDTH].reshape(batch, seq, B_HEADS, hd), (0, 2, 3, 1))
    gc_t = jnp.transpose(gc, (0, 2, 1))

    def blk(i, rev):
        return nb - 1 - i if rev else i

    in_specs = []
    for rev in (False, True):
        col = lambda c0, rev=rev: pl.BlockSpec((1, rows, hd), lambda b, h, i: (b, blk(i, rev), c0 + h))
        gsp = pl.BlockSpec((1, rows, nh), lambda b, h, i, rev=rev: (b, blk(i, rev), 0))
        in_specs += [col(0), col(B_HEADS), col(2 * B_HEADS),
                     pl.BlockSpec((1, 1, hd, rows), lambda b, h, i, rev=rev: (b, h, 0, blk(i, rev))),
                     gsp, gsp,
                     pl.BlockSpec((1, nh, rows), lambda b, h, i, rev=rev: (b, 0, blk(i, rev)))]
    out_specs = [pl.BlockSpec((1, rows, hd), lambda b, h, i: (b, i, h)),
                 pl.BlockSpec((1, rows, hd), lambda b, h, i: (b, nb - 1 - i, h))]
    out_sd = jax.ShapeDtypeStruct((batch, seq, B_WIDTH), F32)
    args = (qkvn, qkvn, qkvn, k_t, gc, beta, gc_t)
    return pl.pallas_call(
        functools.partial(_gdn_kernel, ngroup=rows // B_GROUP),
        grid=(batch, B_HEADS, nb),
        in_specs=in_specs,
        out_specs=out_specs,
        out_shape=[out_sd, out_sd],
        scratch_shapes=[pltpu.VMEM((hd, hd), F32), pltpu.VMEM((hd, hd), F32)],
        compiler_params=_params(("parallel", "parallel", "arbitrary"), 24 << 20),
        name="gdn_scan",
    )(*args, *args)


def _gdn_out_kernel(of_ref, ob_ref, z_ref, g_ref, o_ref):
    for h in range(B_HEADS):
        c0 = h * B_HEAD_DIM
        o = of_ref[:, c0:c0 + B_HEAD_DIM] + ob_ref[:, c0:c0 + B_HEAD_DIM]
        ms = jnp.mean(o * o, axis=-1, keepdims=True)
        z = z_ref[:, c0:c0 + B_HEAD_DIM]
        o_ref[:, c0:c0 + B_HEAD_DIM] = (o * lax.rsqrt(ms + EPS) * g_ref[...] * (z * jax.nn.sigmoid(z))).astype(o_ref.dtype)


def _gdn_out(o_f, o_b, qkvz, o_gain):
    m = o_f.shape[0]
    tm = _pick(m, 512, SUBLANE)
    blk = lambda j: pl.BlockSpec((tm, B_WIDTH), lambda i: (i, j))
    return pl.pallas_call(
        _gdn_out_kernel,
        grid=(m // tm,),
        in_specs=[blk(0), blk(0), blk(3), pl.BlockSpec((1, B_HEAD_DIM), lambda i: (0, 0))],
        out_specs=blk(0),
        out_shape=jax.ShapeDtypeStruct((m, B_WIDTH), BF16),
        compiler_params=_params(("parallel",), 2 * tm * B_WIDTH * 14 + 4 * tm * B_WIDTH * 4),
        name="gdn_out",
    )(o_f, o_b, qkvz, o_gain.reshape(1, B_HEAD_DIM))


def _rope_tables_kernel(pos_ref, f_ref, cos_ref, sin_ref):
    ang = pos_ref[...].astype(F32) * f_ref[...]
    cos_ref[...] = jnp.cos(ang)
    sin_ref[...] = jnp.sin(ang)


def _rope_tables(positions):
    m = positions.size
    inv_freq = ROPE_THETA ** (-jnp.arange(0, C_ROPE, 2, dtype=F32) / C_ROPE)
    f = jnp.tile(inv_freq, LANE // (C_ROPE // 2)).reshape(1, LANE)
    tm = _pick(m, 1024, SUBLANE)
    sd = jax.ShapeDtypeStruct((m, LANE), F32)
    return pl.pallas_call(
        _rope_tables_kernel,
        grid=(m // tm,),
        in_specs=[pl.BlockSpec((tm, 1), lambda i: (i, 0)), pl.BlockSpec((1, LANE), lambda i: (0, 0))],
        out_specs=[pl.BlockSpec((tm, LANE), lambda i: (i, 0))] * 2,
        out_shape=[sd, sd],
        compiler_params=_params(("parallel",), 4 << 20),
        name="rope_tables",
    )(positions.reshape(m, 1), f)


def _rope_slab(t, cos, sin):
    half = C_ROPE // 2
    lane = lax.broadcasted_iota(jnp.int32, t.shape, 1)
    lo = lane < half
    mid = jnp.logical_and(lane >= half, lane < C_ROPE)
    t2_to_lo = pltpu.roll(t, LANE - half, 1)
    t1_to_mid = pltpu.roll(t, half, 1)
    return jnp.where(lo, t * cos - t2_to_lo * sin, jnp.where(mid, t1_to_mid * sin + t * cos, 0.0))


def _q_prep_kernel(q_ref, cos_ref, sin_ref, o_ref, *, scale):
    q = q_ref[0]
    qr = _rope_slab(q[:, C_NOPE:C_QK_PAD], cos_ref[0], sin_ref[0])
    o_ref[0, 0] = (jnp.concatenate([q[:, 0:C_NOPE], qr], axis=1) * scale).astype(o_ref.dtype)


def _q_prep(q, cos, sin, batch, seq):
    tm = _pick(seq, 512, SUBLANE)
    scale = (C_NOPE + C_ROPE) ** -0.5 * math.log2(math.e)
    return pl.pallas_call(
        functools.partial(_q_prep_kernel, scale=scale),
        grid=(batch, seq // tm, C_HEADS),
        in_specs=[pl.BlockSpec((1, tm, C_QK_PAD), lambda b, i, h: (b, i, h)),
                  pl.BlockSpec((1, tm, LANE), lambda b, i, h: (b, i, 0)),
                  pl.BlockSpec((1, tm, LANE), lambda b, i, h: (b, i, 0))],
        out_specs=pl.BlockSpec((1, 1, tm, C_QK_PAD), lambda b, i, h: (b, h, i, 0)),
        out_shape=jax.ShapeDtypeStruct((batch, C_HEADS, seq, C_QK_PAD), BF16),
        compiler_params=_params(("parallel", "parallel", "parallel"), 8 << 20),
        name="mla_q_prep",
    )(q.reshape(batch, seq, C_HEADS * C_QK_PAD), cos.reshape(batch, seq, LANE), sin.reshape(batch, seq, LANE))


def _kv_prep_kernel(kv_ref, kr_ref, cos_ref, sin_ref, kt_ref, v1_ref):
    kv = kv_ref[0]
    lane = lax.broadcasted_iota(jnp.int32, (kv.shape[0], LANE), 1)
    kr = _rope_slab(jnp.where(lane < C_ROPE, kr_ref[0][:, 0:LANE], 0.0), cos_ref[0], sin_ref[0])
    kt_ref[0, 0] = jnp.transpose(jnp.concatenate([kv[:, 0:C_NOPE], kr], axis=1)).astype(kt_ref.dtype)
    v1_ref[0, 0] = jnp.concatenate([kv[:, C_NOPE:C_NOPE + C_V], jnp.ones((kv.shape[0], LANE), F32)],
                                   axis=1).astype(v1_ref.dtype)


def _kv_prep(kv, dkv_small, cos, sin, batch, seq):
    tm = _pick(seq, 512, LANE)
    r3 = lambda t: t.reshape(batch, seq, t.shape[-1])
    small_blk = C_KV_LORA // LANE
    return pl.pallas_call(
        _kv_prep_kernel,
        grid=(batch, seq // tm, C_HEADS),
        in_specs=[pl.BlockSpec((1, tm, C_NOPE + C_V), lambda b, i, h: (b, i, h)),
                  pl.BlockSpec((1, tm, LANE), lambda b, i, h: (b, i, small_blk)),
                  pl.BlockSpec((1, tm, LANE), lambda b, i, h: (b, i, 0)),
                  pl.BlockSpec((1, tm, LANE), lambda b, i, h: (b, i, 0))],
        out_specs=[pl.BlockSpec((1, 1, C_QK_PAD, tm), lambda b, i, h: (b, h, 0, i)),
                   pl.BlockSpec((1, 1, tm, C_V + LANE), lambda b, i, h: (b, h, i, 0))],
        out_shape=[jax.ShapeDtypeStruct((batch, C_HEADS, C_QK_PAD, seq), BF16),
                   jax.ShapeDtypeStruct((batch, C_HEADS, seq, C_V + LANE), BF16)],
        compiler_params=_params(("parallel", "parallel", "parallel"), 8 << 20),
        name="mla_kv_prep",
    )(r3(kv), r3(dkv_small), r3(cos), r3(sin))


def _attn_kernel(q_ref, kt_ref, v1_ref, o_ref, s_ref, m_ref, acc_ref, *, tkv, nsub):
    tq = q_ref.shape[2]
    ts = tq // nsub
    nkv = kt_ref.shape[3] // tkv
    rep = tkv // LANE

    def scores_into(slot, blk):
        k0 = pl.multiple_of(blk * tkv, tkv)
        kt = kt_ref[0, 0, :, pl.ds(k0, tkv)]
        for s in range(nsub):
            s_ref[slot, s] = _dot(q_ref[0, 0, s * ts:(s + 1) * ts, :], kt)

    def consume(slot, blk):
        k0 = pl.multiple_of(blk * tkv, tkv)
        v1 = v1_ref[0, 0, pl.ds(k0, tkv), :]
        for s in range(nsub):
            sc = s_ref[slot, s]
            m = m_ref[s]
            m_new = jnp.maximum(m, jnp.max(sc, axis=1, keepdims=True))
            alpha = jnp.exp2(m - m_new)
            p = jnp.exp2(sc - pltpu.repeat(m_new, rep, 1)).astype(BF16)
            acc_ref[s] = pltpu.repeat(alpha, 2, 1) * acc_ref[s] + _dot(p, v1)
            m_ref[s] = m_new

    m_ref[...] = jnp.full(m_ref.shape, -jnp.inf, F32)
    acc_ref[...] = jnp.zeros(acc_ref.shape, F32)
    scores_into(0, 0)

    def pair(jj, carry):
        b0 = 2 * jj
        scores_into(1, b0 + 1)
        consume(0, b0)
        scores_into(0, jnp.minimum(b0 + 2, nkv - 1))
        consume(1, b0 + 1)
        return carry

    lax.fori_loop(0, nkv // 2, pair, 0)
    if nkv % 2 == 1:
        consume(0, nkv - 1)
    for s in range(nsub):
        acc = acc_ref[s]
        o_ref[0, s * ts:(s + 1) * ts, :] = (acc[:, 0:C_V] / acc[:, C_V:C_V + LANE]).astype(o_ref.dtype)


def _attention(q, kt, v1):
    batch, heads, seq, _ = q.shape
    tq = _pick(seq, 1024, LANE)
    nsub = 2 if tq % (2 * LANE) == 0 else 1
    ts = tq // nsub
    tkv = _pick(seq, 1024, LANE)
    est = 2 * (tq * C_QK_PAD * 2 + 2 * seq * C_QK_PAD * 2 + tq * C_V * 2) + 8 * tkv * tq * 4
    return pl.pallas_call(
        functools.partial(_attn_kernel, tkv=tkv, nsub=nsub),
        grid=(batch, heads, seq // tq),
        in_specs=[pl.BlockSpec((1, 1, tq, C_QK_PAD), lambda b, h, i: (b, h, i, 0)),
                  pl.BlockSpec((1, 1, C_QK_PAD, seq), lambda b, h, i: (b, h, 0, 0)),
                  pl.BlockSpec((1, 1, seq, C_V + LANE), lambda b, h, i: (b, h, 0, 0))],
        out_specs=pl.BlockSpec((1, tq, C_V), lambda b, h, i: (b, i, h)),
        out_shape=jax.ShapeDtypeStruct((batch, seq, heads * C_V), BF16),
        scratch_shapes=[pltpu.VMEM((2, nsub, ts, tkv), F32), pltpu.VMEM((nsub, ts, LANE), F32),
                        pltpu.VMEM((nsub, ts, C_V + LANE), F32)],
        compiler_params=_params(("parallel", "parallel", "parallel"), est),
        name="mla_attention",
    )(q, kt, v1)


def _pad_cols(w, n):
    return jnp.pad(w, ((0, 0), (0, n - w.shape[1])))


def _layer_weights(p, l, d_ff, d_ffp):
    bf = lambda t: t.astype(BF16)
    o = {}
    for tag in ("ffn1", "ffn2"):
        up = p[tag + "_up"][l]
        o[tag + "_wg"] = _pad_cols(bf(up[:, :d_ff]), d_ffp)
        o[tag + "_wu"] = _pad_cols(bf(up[:, d_ff:]), d_ffp)
        o[tag + "_wd"] = jnp.pad(bf(p[tag + "_down"][l]), ((0, d_ffp - d_ff), (0, 0)))
    w_in = p["w_in"][l]
    c0 = 0
    seg = {}
    for name, width in (("a", 2 * A_WIDTH), ("qkvz", 4 * B_WIDTH), ("ab", 4 * B_HEADS), ("dq", C_Q_LORA),
                        ("dkv", C_KV_LORA), ("kr", C_ROPE)):
        seg[name] = w_in[:, c0:c0 + width]
        c0 += width
    o["w_a"] = bf(seg["a"])
    o["w_qkvz"] = bf(seg["qkvz"])
    o["w_dq"] = bf(seg["dq"])
    o["w_dkv_small"] = bf(jnp.concatenate([seg["dkv"], seg["kr"], seg["ab"]], axis=1))
    uq = p["c_w_uq"][l].reshape(C_Q_LORA, C_HEADS, C_NOPE + C_ROPE)
    o["w_uq"] = bf(jnp.pad(uq, ((0, 0), (0, 0), (0, C_QK_PAD - C_NOPE - C_ROPE))).reshape(C_Q_LORA, C_HEADS * C_QK_PAD))
    o["w_ukv"] = bf(p["c_w_ukv"][l])
    o["w_gate"] = bf(p["w_gate"][l])
    o["w_branch"] = bf(p["w_branch"][l])
    o["w_o"] = bf(p["w_o"][l])
    return o


def kernel(x, positions, norm_ffn1, ffn1_up, ffn1_down, norm_mix, w_in, w_gate, a_v_gain, a_w_s, a_b_s, b_conv,
           b_a_log, b_dt_bias, b_o_gain, c_q_gain, c_kv_gain, c_w_uq, c_w_ukv, w_branch, w_o, norm_ffn2, ffn2_up,
           ffn2_down, norm_final):
    batch, seq, d = x.shape
    depth = norm_ffn1.shape[0]
    d_ff = ffn1_down.shape[1]
    d_ffp = -(-d_ff // 1024) * 1024 if d_ff > 1024 else -(-d_ff // LANE) * LANE
    m = batch * seq
    p = dict(ffn1_up=ffn1_up, ffn1_down=ffn1_down, ffn2_up=ffn2_up, ffn2_down=ffn2_down, w_in=w_in, w_gate=w_gate,
             c_w_uq=c_w_uq, c_w_ukv=c_w_ukv, w_branch=w_branch, w_o=w_o)

    cos, sin = _rope_tables(positions)
    xs = x.reshape(m, d)
    for l in range(depth):
        w = _layer_weights(p, l, d_ff, d_ffp)

        hn = _rmsnorm(xs, norm_ffn1[l], BF16)
        xs = _mm_res(_ffn_up(hn, w["ffn1_wg"], w["ffn1_wu"]), w["ffn1_wd"], xs, 0.5, name="ffn_down")

        h = _rmsnorm(xs, norm_mix[l], BF16)
        guv = _mm(h, w["w_a"], F32, gelu=True, name="proj_a")
        qkvz = _mm(h, w["w_qkvz"], F32, name="proj_b")
        dq = _mm(h, w["w_dq"], F32, name="proj_dq")
        dkv_small = _mm(h, w["w_dkv_small"], F32, name="proj_dkv")

        y_a = _gmlp(guv, a_v_gain[l], a_w_s[l], a_b_s[l])

        gc, beta = _gdn_gate(dkv_small, b_a_log[l], b_dt_bias[l])
        qkvn = _conv_qkv(qkvz, b_conv[l], batch, seq)
        nh = 2 * B_HEADS
        o_f, o_b = _gdn(qkvn, gc.reshape(batch, seq, nh), beta.reshape(batch, seq, nh))
        y_b = _gdn_out(o_f.reshape(m, B_WIDTH), o_b.reshape(m, B_WIDTH), qkvz, b_o_gain[l])

        q = _norm_mm(dq, c_q_gain[l], w["w_uq"], F32, name="mla_uq")
        kv = _norm_mm(dkv_small, c_kv_gain[l], w["w_ukv"], F32, name="mla_ukv")
        qh = _q_prep(q, cos, sin, batch, seq)
        kt, v1 = _kv_prep(kv, dkv_small, cos, sin, batch, seq)
        y_c = _attention(qh, kt, v1).reshape(m, C_WIDTH)

        merged = _merge(h, jnp.stack([y_a, y_b, y_c], axis=0), w["w_gate"], w["w_branch"])
        xs = _mm_res(merged, w["w_o"], xs, 1.0, name="out_proj")

        hn = _rmsnorm(xs, norm_ffn2[l], BF16)
        xs = _mm_res(_ffn_up(hn, w["ffn2_wg"], w["ffn2_wu"]), w["ffn2_wd"], xs, 0.5, name="ffn_down")

    return _rmsnorm(xs, norm_final, F32).reshape(batch, seq, d)
```

```python
import functools
import math

import jax
import jax.numpy as jnp
from jax import lax
from jax.experimental import pallas as pl
from jax.experimental.pallas import tpu as pltpu

F32 = jnp.float32
BF16 = jnp.bfloat16
EPS = 1e-6
ROPE_THETA = 10000.0

A_GROUPS, A_GROUP_DIM, A_CHUNK = 16, 128, 128
A_WIDTH = A_GROUPS * A_GROUP_DIM
B_HEADS, B_HEAD_DIM, B_CONV, B_CHUNK = 16, 128, 5, 64
B_WIDTH = B_HEADS * B_HEAD_DIM
B_GROUP = 256
C_HEADS, C_Q_LORA, C_KV_LORA, C_NOPE, C_ROPE, C_V = 16, 1024, 512, 128, 64, 128
C_WIDTH = C_HEADS * C_V
C_QK_PAD = 256
N_BRANCH = 3

VMEM_BUDGET_BYTES = 44 * 1024 * 1024
VMEM_LIMIT_CAP_BYTES = 56 * 1024 * 1024
LANE = 128
SUBLANE = 8
HIGHEST = lax.Precision.HIGHEST


def _pick(n, pref, align):
    if n <= pref:
        return n
    t = (pref // align) * align
    while t >= align:
        if n % t == 0:
            return t
        t -= align
    return n


def _params(sem, vmem_bytes):
    limit = int(min(VMEM_LIMIT_CAP_BYTES, max(vmem_bytes + (8 << 20), 32 << 20)))
    return pltpu.CompilerParams(dimension_semantics=sem, vmem_limit_bytes=limit)


def _dot(a, b):
    return jnp.dot(a, b, preferred_element_type=F32)


def _dot_f32(a, b):
    return jnp.dot(a, b, preferred_element_type=F32, precision=HIGHEST)


def _rmsnorm_kernel(x_ref, g_ref, o_ref):
    x = x_ref[...]
    ms = jnp.mean(x * x, axis=-1, keepdims=True)
    o_ref[...] = (x * lax.rsqrt(ms + EPS) * g_ref[...]).astype(o_ref.dtype)


def _rmsnorm(x, gain, out_dtype):
    m, d = x.shape
    tm = _pick(m, 512, SUBLANE)
    return pl.pallas_call(
        _rmsnorm_kernel,
        grid=(m // tm,),
        in_specs=[pl.BlockSpec((tm, d), lambda i: (i, 0)), pl.BlockSpec((1, d), lambda i: (0, 0))],
        out_specs=pl.BlockSpec((tm, d), lambda i: (i, 0)),
        out_shape=jax.ShapeDtypeStruct((m, d), out_dtype),
        compiler_params=_params(("parallel",), 4 * tm * d * 6),
        name="rmsnorm",
    )(x, gain.reshape(1, d))


def _mm_kernel(a_ref, w_ref, o_ref, *, gelu):
    y = _dot(a_ref[...], w_ref[0])
    if gelu:
        y = 0.5 * y * (1.0 + lax.erf(y * math.sqrt(0.5)))
    o_ref[...] = y.astype(o_ref.dtype)


def _mm(a, w, layer, col0, n, out_dtype, gelu=False, name="mm"):
    m, k = a.shape
    osz = jnp.dtype(out_dtype).itemsize
    tm = _pick(m, 1024, SUBLANE)
    tn = _pick(math.gcd(n, col0) if col0 else n, 1024, LANE)
    est = lambda tn_: 2 * (tm * k * 2 + k * tn_ * 2 + tm * tn_ * osz) + tm * tn_ * 4
    while est(tn) > VMEM_BUDGET_BYTES and tn % (2 * LANE) == 0:
        tn //= 2
    jb = col0 // tn
    return pl.pallas_call(
        functools.partial(_mm_kernel, gelu=gelu),
        grid=(m // tm, n // tn),
        in_specs=[pl.BlockSpec((tm, k), lambda i, j: (i, 0)), pl.BlockSpec((1, k, tn), lambda i, j: (layer, 0, jb + j))],
        out_specs=pl.BlockSpec((tm, tn), lambda i, j: (i, j)),
        out_shape=jax.ShapeDtypeStruct((m, n), out_dtype),
        compiler_params=_params(("parallel", "parallel"), est(tn)),
        name=name,
    )(a, w)


def _norm_mm_kernel(a_ref, g_ref, w_ref, o_ref):
    x = a_ref[...]
    ms = jnp.mean(x * x, axis=-1, keepdims=True)
    xn = (x * lax.rsqrt(ms + EPS) * g_ref[...]).astype(BF16)
    o_ref[...] = _dot(xn, w_ref[0]).astype(o_ref.dtype)


def _norm_mm(a, gain, w, layer, out_dtype, name="norm_mm"):
    m = a.shape[0]
    _, k, n = w.shape
    tm = _pick(m, 1024, SUBLANE)
    tn = _pick(n, 1024, LANE)
    est = 2 * (tm * k * 4 + k * tn * 2 + tm * tn * 4) + tm * k * 6 + tm * tn * 4
    return pl.pallas_call(
        _norm_mm_kernel,
        grid=(m // tm, n // tn),
        in_specs=[pl.BlockSpec((tm, k), lambda i, j: (i, 0)), pl.BlockSpec((1, k), lambda i, j: (0, 0)),
                  pl.BlockSpec((1, k, tn), lambda i, j: (layer, 0, j))],
        out_specs=pl.BlockSpec((tm, tn), lambda i, j: (i, j)),
        out_shape=jax.ShapeDtypeStruct((m, n), out_dtype),
        compiler_params=_params(("parallel", "parallel"), est),
        name=name,
    )(a, gain.reshape(1, k), w)


def _ffn_up_kernel(a_ref, wg_ref, wu_ref, o_ref):
    a = a_ref[...]
    g = _dot(a, wg_ref[0])
    u = _dot(a, wu_ref[0])
    o_ref[...] = (g * jax.nn.sigmoid(g) * u).astype(o_ref.dtype)


def _ffn_up(a, w_up, layer):
    m, k = a.shape
    n = w_up.shape[2] // 2
    tm = _pick(m, 1024, SUBLANE)
    tn = _pick(n, 512, LANE)
    nj = n // tn
    est = 2 * (tm * k * 2 + 2 * k * tn * 2 + tm * tn * 2) + 3 * tm * tn * 4
    return pl.pallas_call(
        _ffn_up_kernel,
        grid=(m // tm, nj),
        in_specs=[pl.BlockSpec((tm, k), lambda i, j: (i, 0)), pl.BlockSpec((1, k, tn), lambda i, j: (layer, 0, j)),
                  pl.BlockSpec((1, k, tn), lambda i, j: (layer, 0, nj + j))],
        out_specs=pl.BlockSpec((tm, tn), lambda i, j: (i, j)),
        out_shape=jax.ShapeDtypeStruct((m, n), BF16),
        compiler_params=_params(("parallel", "parallel"), est),
        name="ffn_up",
    )(a, w_up, w_up)


def _mm_res_kernel(a_ref, w_ref, r_ref, o_ref, *, scale):
    o_ref[...] = r_ref[...] + scale * _dot(a_ref[...], w_ref[0])


def _mm_res(a, w, layer, res, scale, name="mm_res"):
    m, k = a.shape
    n = w.shape[2]
    tm = _pick(m, 1024, SUBLANE)
    a_bufs = 2 if 2 * tm * k * 2 <= VMEM_BUDGET_BYTES // 2 else 1
    tn = _pick(n, 1024, LANE)
    est = lambda tn_: a_bufs * tm * k * 2 + 2 * (k * tn_ * 2 + 2 * tm * tn_ * 4) + tm * tn_ * 4
    while est(tn) > VMEM_BUDGET_BYTES and tn % (2 * LANE) == 0:
        tn //= 2
    a_spec = pl.BlockSpec((tm, k), lambda i, j: (i, 0), pipeline_mode=pl.Buffered(a_bufs))
    return pl.pallas_call(
        functools.partial(_mm_res_kernel, scale=scale),
        grid=(m // tm, n // tn),
        in_specs=[a_spec, pl.BlockSpec((1, k, tn), lambda i, j: (layer, 0, j)),
                  pl.BlockSpec((tm, tn), lambda i, j: (i, j))],
        out_specs=pl.BlockSpec((tm, tn), lambda i, j: (i, j)),
        out_shape=jax.ShapeDtypeStruct((m, n), F32),
        compiler_params=_params(("parallel", "arbitrary"), est(tn)),
        name=name,
    )(a, w, res)


def _merge_kernel(h_ref, y_ref, wg_ref, wb_ref, o_ref, acc_ref):
    br = pl.program_id(2)
    gate = jax.nn.sigmoid(_dot(h_ref[...], wg_ref[0]))
    term = gate * _dot(y_ref[0], wb_ref[0, 0])

    @pl.when(br == 0)
    def _():
        acc_ref[...] = term

    @pl.when(br == 1)
    def _():
        acc_ref[...] += term

    @pl.when(br == N_BRANCH - 1)
    def _():
        o_ref[...] = (acc_ref[...] + term).astype(o_ref.dtype)


def _merge(h, ys, w_gate, w_branch, layer):
    m, d = h.shape
    wdt = ys.shape[2]
    tm = _pick(m, 1024, SUBLANE)
    tn = _pick(d, 512, LANE)
    nj = d // tn
    est = 2 * (tm * d * 2 + tm * wdt * 2 + d * tn * 2 + wdt * tn * 2 + tm * tn * 2) + 4 * tm * tn * 4
    return pl.pallas_call(
        _merge_kernel,
        grid=(m // tm, nj, N_BRANCH),
        in_specs=[pl.BlockSpec((tm, d), lambda i, j, b: (i, 0)),
                  pl.BlockSpec((1, tm, wdt), lambda i, j, b: (b, i, 0)),
                  pl.BlockSpec((1, d, tn), lambda i, j, b: (layer, 0, b * nj + j)),
                  pl.BlockSpec((1, 1, wdt, tn), lambda i, j, b: (layer, b, 0, j))],
        out_specs=pl.BlockSpec((tm, tn), lambda i, j, b: (i, j)),
        out_shape=jax.ShapeDtypeStruct((m, d), BF16),
        scratch_shapes=[pltpu.VMEM((tm, tn), F32)],
        compiler_params=_params(("parallel", "parallel", "arbitrary"), est),
        name="merge",
    )(h, ys, w_gate, w_branch)


def _gmlp_kernel(u_ref, v_ref, g_ref, ws_ref, bt_ref, o_ref):
    v = v_ref[...]
    ms = jnp.mean(v * v, axis=-1, keepdims=True)
    vn = (v * lax.rsqrt(ms + EPS) * g_ref[...]).astype(BF16)
    rows = v.shape[0]
    for c in range(rows // A_CHUNK):
        r0 = c * A_CHUNK
        for g in range(A_GROUPS):
            c0 = g * A_GROUP_DIM
            s = _dot(ws_ref[0, g], vn[r0:r0 + A_CHUNK, c0:c0 + A_GROUP_DIM]) + bt_ref[:, g:g + 1]
            o_ref[r0:r0 + A_CHUNK, c0:c0 + A_GROUP_DIM] = (
                u_ref[r0:r0 + A_CHUNK, c0:c0 + A_GROUP_DIM] * s).astype(o_ref.dtype)


def _gmlp(guv, v_gain, w_s, layer, b_s):
    m = guv.shape[0]
    tm = _pick(m, 256, A_CHUNK)
    est = 2 * (2 * tm * A_WIDTH * 4 + tm * A_WIDTH * 2) + 3 * tm * A_WIDTH * 4
    return pl.pallas_call(
        _gmlp_kernel,
        grid=(m // tm,),
        in_specs=[pl.BlockSpec((tm, A_WIDTH), lambda i: (i, 0)), pl.BlockSpec((tm, A_WIDTH), lambda i: (i, 1)),
                  pl.BlockSpec((1, A_WIDTH), lambda i: (0, 0)),
                  pl.BlockSpec((1, A_GROUPS, A_CHUNK, A_CHUNK), lambda i: (layer, 0, 0, 0)),
                  pl.BlockSpec((A_CHUNK, A_GROUPS), lambda i: (0, 0))],
        out_specs=pl.BlockSpec((tm, A_WIDTH), lambda i: (i, 0)),
        out_shape=jax.ShapeDtypeStruct((m, A_WIDTH), BF16),
        compiler_params=_params(("parallel",), est),
        name="gmlp",
    )(guv, guv, v_gain.reshape(1, A_WIDTH), w_s, b_s.T)


def _gdn_gate_kernel(s_ref, alog_ref, dtb_ref, gc_ref, beta_ref):
    x = s_ref[...]
    a = x[:, C_ROPE:C_ROPE + 2 * B_HEADS]
    b = x[:, C_ROPE + 2 * B_HEADS:C_ROPE + 4 * B_HEADS]
    z = a + dtb_ref[...]
    softplus = jnp.maximum(z, 0.0) + jnp.log1p(jnp.exp(-jnp.abs(z)))
    g = -jnp.exp(alog_ref[...]) * softplus
    beta_ref[...] = jax.nn.sigmoid(b)
    ri = lax.broadcasted_iota(jnp.int32, (B_CHUNK, B_CHUNK), 0)
    ci = lax.broadcasted_iota(jnp.int32, (B_CHUNK, B_CHUNK), 1)
    tril = (ri >= ci).astype(F32)
    triu = (ri <= ci).astype(F32)
    fwd = lax.broadcasted_iota(jnp.int32, (B_CHUNK, 2 * B_HEADS), 1) < B_HEADS
    for c in range(x.shape[0] // B_CHUNK):
        gcnk = g[c * B_CHUNK:(c + 1) * B_CHUNK]
        gc_ref[c * B_CHUNK:(c + 1) * B_CHUNK, :] = jnp.where(fwd, _dot_f32(tril, gcnk), _dot_f32(triu, gcnk))


def _gdn_gate(dkv_small, a_log, dt_bias):
    m = dkv_small.shape[0]
    tm = _pick(m, 512, B_CHUNK)
    nh = 2 * B_HEADS
    small_blk = C_KV_LORA // LANE
    return pl.pallas_call(
        _gdn_gate_kernel,
        grid=(m // tm,),
        in_specs=[pl.BlockSpec((tm, LANE), lambda i: (i, small_blk)), pl.BlockSpec((1, nh), lambda i: (0, 0)),
                  pl.BlockSpec((1, nh), lambda i: (0, 0))],
        out_specs=[pl.BlockSpec((tm, nh), lambda i: (i, 0)), pl.BlockSpec((tm, nh), lambda i: (i, 0))],
        out_shape=[jax.ShapeDtypeStruct((m, nh), F32), jax.ShapeDtypeStruct((m, nh), F32)],
        compiler_params=_params(("parallel",), 1 << 20),
        name="gdn_gate",
    )(dkv_small, a_log.reshape(1, nh), dt_bias.reshape(1, nh))


def _conv_kernel(x_ref, p_ref, n_ref, w_ref, o_ref, xe_ref, *, ts, nblk):
    i = pl.program_id(1)
    j = pl.program_id(2)
    halo = SUBLANE
    xe_ref[0:halo, :] = jnp.where(i > 0, p_ref[0], 0.0)
    xe_ref[halo:halo + ts, :] = x_ref[0]
    xe_ref[halo + ts:halo + ts + halo, :] = jnp.where(i < nblk - 1, n_ref[0], 0.0)
    pad = (B_CONV - 1) // 2
    acc = None
    for t in range(B_CONV):
        term = w_ref[t:t + 1, :] * xe_ref[pl.ds(halo - pad + t, ts), :]
        acc = term if acc is None else acc + term
    y = acc * jax.nn.sigmoid(acc)
    qscale = jnp.where(j == 0, B_HEAD_DIM ** -0.5, 1.0).astype(F32)
    for h in range(B_HEADS):
        c0 = h * B_HEAD_DIM
        yh = y[:, c0:c0 + B_HEAD_DIM]
        ss = jnp.sum(yh * yh, axis=-1, keepdims=True)
        nrm = jnp.where(j < 2, lax.rsqrt(ss + EPS) * qscale, 1.0)
        o_ref[0, :, c0:c0 + B_HEAD_DIM] = yh * nrm


def _conv_qkv(qkvz, conv_w, batch, seq):
    x = qkvz.reshape(batch, seq, 4 * B_WIDTH)
    ts = _pick(seq, 256, SUBLANE)
    nblk = seq // ts
    hb = ts // SUBLANE
    nh8 = seq // SUBLANE
    tc = B_WIDTH
    est = 2 * (2 * ts * tc * 4 + 2 * SUBLANE * tc * 4) + 4 * (ts + 16) * tc * 4
    return pl.pallas_call(
        functools.partial(_conv_kernel, ts=ts, nblk=nblk),
        grid=(batch, nblk, 3),
        in_specs=[pl.BlockSpec((1, ts, tc), lambda b, i, j: (b, i, j)),
                  pl.BlockSpec((1, SUBLANE, tc), lambda b, i, j: (b, jnp.maximum(i * hb - 1, 0), j)),
                  pl.BlockSpec((1, SUBLANE, tc), lambda b, i, j: (b, jnp.minimum((i + 1) * hb, nh8 - 1), j)),
                  pl.BlockSpec((B_CONV, tc), lambda b, i, j: (0, j))],
        out_specs=pl.BlockSpec((1, ts, tc), lambda b, i, j: (b, i, j)),
        out_shape=jax.ShapeDtypeStruct((batch, seq, 3 * B_WIDTH), F32),
        scratch_shapes=[pltpu.VMEM((ts + 2 * SUBLANE, tc), F32)],
        compiler_params=_params(("parallel", "parallel", "parallel"), est),
        name="gdn_conv",
    )(x, x, x, conv_w)


def _gdn_kernel(qf_ref, kf_ref, vf_ref, gf_ref, bf_ref, gtf_ref,
                qb_ref, kb_ref, vb_ref, gb_ref, bb_ref, gtb_ref,
                of_ref, ob_ref, sf_ref, sb_ref, *, ngroup):
    h = pl.program_id(1)
    i = pl.program_id(2)

    @pl.when(i == 0)
    def _():
        sf_ref[...] = jnp.zeros_like(sf_ref)
        sb_ref[...] = jnp.zeros_like(sb_ref)

    g_rows, c_rows, dk = B_GROUP, B_CHUNK, B_HEAD_DIM
    cpg = g_rows // c_rows
    rid = lax.broadcasted_iota(jnp.int32, (g_rows, g_rows), 0)
    cid = lax.broadcasted_iota(jnp.int32, (g_rows, g_rows), 1)
    same_chunk = (rid // c_rows) == (cid // c_rows)
    eye = (rid == cid).astype(F32)
    lane = lax.broadcasted_iota(jnp.int32, (1, 2 * B_HEADS), 1)
    dirs = (
        dict(q=qf_ref, k=kf_ref, v=vf_ref, g=gf_ref, b=bf_ref, gt=gtf_ref, o=of_ref, s=sf_ref,
             col=h, incl=jnp.logical_and(same_chunk, rid >= cid), strict=jnp.logical_and(same_chunk, rid > cid),
             last=c_rows - 1, rev=False),
        dict(q=qb_ref, k=kb_ref, v=vb_ref, g=gb_ref, b=bb_ref, gt=gtb_ref, o=ob_ref, s=sb_ref,
             col=h + B_HEADS, incl=jnp.logical_and(same_chunk, rid <= cid),
             strict=jnp.logical_and(same_chunk, rid < cid), last=0, rev=True),
    )

    units = []
    for d in dirs:
        sel = lane == d["col"]
        grow_all = d["gt"][0, pl.ds(d["col"], 1), :]
        order = range(ngroup - 1, -1, -1) if d["rev"] else range(ngroup)
        for gi in order:
            r0 = gi * g_rows
            q = d["q"][0, r0:r0 + g_rows, :]
            k = d["k"][0, r0:r0 + g_rows, :]
            v = d["v"][0, r0:r0 + g_rows, :]
            gcol = jnp.sum(jnp.where(sel, d["g"][0, r0:r0 + g_rows, :], 0.0), axis=-1, keepdims=True)
            bcol = jnp.sum(jnp.where(sel, d["b"][0, r0:r0 + g_rows, :], 0.0), axis=-1, keepdims=True)
            grow = grow_all[:, r0:r0 + g_rows]
            eg = jnp.exp(gcol)
            kbeta = k * bcol
            decay = jnp.where(d["incl"], jnp.exp(jnp.where(d["incl"], gcol - grow, 0.0)), 0.0)
            x = lax.dot_general(jnp.concatenate([kbeta, q], axis=0).astype(BF16), k.astype(BF16),
                                (((1,), (1,)), ((), ())), preferred_element_type=F32)
            n = -jnp.where(d["strict"], x[0:g_rows] * decay, 0.0)
            units.append(dict(d=d, r0=r0, q=q, k=k, eg=eg, gcol=gcol, n=n, attn=x[g_rows:] * decay,
                              rhs=jnp.concatenate([v * bcol, kbeta * eg], axis=1)))

    for u in units:
        u["nb"] = u["n"].astype(BF16)
        u["p"] = eye + u["n"]
    for _ in range(int(math.log2(c_rows)) - 1):
        for u in units:
            u["nb"] = _dot(u["nb"], u["nb"]).astype(BF16)
        for u in units:
            u["p"] = u["p"] + _dot(u["p"].astype(BF16), u["nb"])

    for u in units:
        u["uw"] = _dot(u["p"].astype(BF16), u["rhs"].astype(BF16)).astype(BF16)
    for u in units:
        au = _dot(u["attn"].astype(BF16), u["uw"])
        u["ob"] = au[:, 0:dk]
        u["qeff"] = u["q"] * u["eg"] - au[:, dk:2 * dk]

    steps = {False: [], True: []}
    for u in units:
        d = u["d"]
        corder = range(cpg - 1, -1, -1) if d["rev"] else range(cpg)
        for c in corder:
            c0 = c * c_rows
            gcolc = u["gcol"][c0:c0 + c_rows]
            glc = gcolc[d["last"]:d["last"] + 1, :]
            k_dec = (u["k"][c0:c0 + c_rows] * jnp.exp(glc - gcolc)).astype(BF16)
            kuw = lax.dot_general(k_dec, u["uw"][c0:c0 + c_rows, :], (((0,), (0,)), ((), ())),
                                  preferred_element_type=F32)
            lhs = jnp.concatenate([-kuw[:, dk:2 * dk], u["qeff"][c0:c0 + c_rows]], axis=0).astype(BF16)
            steps[d["rev"]].append(dict(lhs=lhs, bmat=kuw[:, 0:dk], egl=jnp.exp(glc),
                                        ob=u["ob"][c0:c0 + c_rows], row=u["r0"] + c0))

    state = {False: sf_ref[...], True: sb_ref[...]}
    for t in range(ngroup * cpg):
        for d in dirs:
            st = steps[d["rev"]][t]
            ys = _dot(st["lhs"], state[d["rev"]].astype(BF16))
            state[d["rev"]] = state[d["rev"]] * st["egl"] + ys[0:dk] + st["bmat"]
            d["o"][0, st["row"]:st["row"] + c_rows, :] = ys[dk:dk + c_rows] + st["ob"]
    sf_ref[...] = state[False]
    sb_ref[...] = state[True]


def _gdn(qkvn, gc, beta):
    batch, seq, _ = qkvn.shape
    rows = _pick(seq, 1024, B_GROUP)
    nb = seq // rows
    nh = 2 * B_HEADS
    hd = B_HEAD_DIM
    gc_t = jnp.transpose(gc, (0, 2, 1))

    def blk(i, rev):
        return nb - 1 - i if rev else i

    in_specs = []
    for rev in (False, True):
        col = lambda c0, rev=rev: pl.BlockSpec((1, rows, hd), lambda b, h, i: (b, blk(i, rev), c0 + h))
        gsp = pl.BlockSpec((1, rows, nh), lambda b, h, i, rev=rev: (b, blk(i, rev), 0))
        in_specs += [col(0), col(B_HEADS), col(2 * B_HEADS), gsp, gsp,
                     pl.BlockSpec((1, nh, rows), lambda b, h, i, rev=rev: (b, 0, blk(i, rev)))]
    out_specs = [pl.BlockSpec((1, rows, hd), lambda b, h, i: (b, i, h)),
                 pl.BlockSpec((1, rows, hd), lambda b, h, i: (b, nb - 1 - i, h))]
    out_sd = jax.ShapeDtypeStruct((batch, seq, B_WIDTH), F32)
    args = (qkvn, qkvn, qkvn, gc, beta, gc_t)
    return pl.pallas_call(
        functools.partial(_gdn_kernel, ngroup=rows // B_GROUP),
        grid=(batch, B_HEADS, nb),
        in_specs=in_specs,
        out_specs=out_specs,
        out_shape=[out_sd, out_sd],
        scratch_shapes=[pltpu.VMEM((hd, hd), F32), pltpu.VMEM((hd, hd), F32)],
        compiler_params=_params(("parallel", "parallel", "arbitrary"), 24 << 20),
        name="gdn_scan",
    )(*args, *args)


def _gdn_out_kernel(of_ref, ob_ref, z_ref, g_ref, o_ref):
    for h in range(B_HEADS):
        c0 = h * B_HEAD_DIM
        o = of_ref[:, c0:c0 + B_HEAD_DIM] + ob_ref[:, c0:c0 + B_HEAD_DIM]
        ms = jnp.mean(o * o, axis=-1, keepdims=True)
        z = z_ref[:, c0:c0 + B_HEAD_DIM]
        o_ref[:, c0:c0 + B_HEAD_DIM] = (o * lax.rsqrt(ms + EPS) * g_ref[...] * (z * jax.nn.sigmoid(z))).astype(o_ref.dtype)


def _gdn_out(o_f, o_b, qkvz, o_gain):
    m = o_f.shape[0]
    tm = _pick(m, 512, SUBLANE)
    blk = lambda j: pl.BlockSpec((tm, B_WIDTH), lambda i: (i, j))
    return pl.pallas_call(
        _gdn_out_kernel,
        grid=(m // tm,),
        in_specs=[blk(0), blk(0), blk(3), pl.BlockSpec((1, B_HEAD_DIM), lambda i: (0, 0))],
        out_specs=blk(0),
        out_shape=jax.ShapeDtypeStruct((m, B_WIDTH), BF16),
        compiler_params=_params(("parallel",), 2 * tm * B_WIDTH * 14 + 4 * tm * B_WIDTH * 4),
        name="gdn_out",
    )(o_f, o_b, qkvz, o_gain.reshape(1, B_HEAD_DIM))


def _rope_tables_kernel(pos_ref, f_ref, cos_ref, sin_ref):
    ang = pos_ref[...].astype(F32) * f_ref[...]
    cos_ref[...] = jnp.cos(ang)
    sin_ref[...] = jnp.sin(ang)


def _rope_tables(positions):
    m = positions.size
    inv_freq = ROPE_THETA ** (-jnp.arange(0, C_ROPE, 2, dtype=F32) / C_ROPE)
    f = jnp.tile(inv_freq, LANE // (C_ROPE // 2)).reshape(1, LANE)
    tm = _pick(m, 1024, SUBLANE)
    sd = jax.ShapeDtypeStruct((m, LANE), F32)
    return pl.pallas_call(
        _rope_tables_kernel,
        grid=(m // tm,),
        in_specs=[pl.BlockSpec((tm, 1), lambda i: (i, 0)), pl.BlockSpec((1, LANE), lambda i: (0, 0))],
        out_specs=[pl.BlockSpec((tm, LANE), lambda i: (i, 0))] * 2,
        out_shape=[sd, sd],
        compiler_params=_params(("parallel",), 4 << 20),
        name="rope_tables",
    )(positions.reshape(m, 1), f)


def _rope_slab(t, cos, sin):
    half = C_ROPE // 2
    lane = lax.broadcasted_iota(jnp.int32, t.shape, 1)
    lo = lane < half
    mid = jnp.logical_and(lane >= half, lane < C_ROPE)
    t2_to_lo = pltpu.roll(t, LANE - half, 1)
    t1_to_mid = pltpu.roll(t, half, 1)
    return jnp.where(lo, t * cos - t2_to_lo * sin, jnp.where(mid, t1_to_mid * sin + t * cos, 0.0))


def _q_prep_kernel(q_ref, cos_ref, sin_ref, o_ref, *, scale):
    cos, sin = cos_ref[0], sin_ref[0]
    for h in range(C_HEADS):
        c0 = h * C_QK_PAD
        qr = _rope_slab(q_ref[0, :, c0 + C_NOPE:c0 + C_QK_PAD], cos, sin)
        o_ref[0, h, :, 0:C_NOPE] = (q_ref[0, :, c0:c0 + C_NOPE] * scale).astype(o_ref.dtype)
        o_ref[0, h, :, C_NOPE:C_QK_PAD] = (qr * scale).astype(o_ref.dtype)


def _q_prep(q, cos, sin, batch, seq):
    tm = _pick(seq, 256, SUBLANE)
    scale = (C_NOPE + C_ROPE) ** -0.5 * math.log2(math.e)
    width = C_HEADS * C_QK_PAD
    return pl.pallas_call(
        functools.partial(_q_prep_kernel, scale=scale),
        grid=(batch, seq // tm),
        in_specs=[pl.BlockSpec((1, tm, width), lambda b, i: (b, i, 0)),
                  pl.BlockSpec((1, tm, LANE), lambda b, i: (b, i, 0)),
                  pl.BlockSpec((1, tm, LANE), lambda b, i: (b, i, 0))],
        out_specs=pl.BlockSpec((1, C_HEADS, tm, C_QK_PAD), lambda b, i: (b, 0, i, 0)),
        out_shape=jax.ShapeDtypeStruct((batch, C_HEADS, seq, C_QK_PAD), BF16),
        compiler_params=_params(("parallel", "parallel"), 2 * tm * width * 6 + 2 * tm * width * 4),
        name="mla_q_prep",
    )(q.reshape(batch, seq, width), cos.reshape(batch, seq, LANE), sin.reshape(batch, seq, LANE))


def _kv_prep_kernel(kv_ref, kr_ref, cos_ref, sin_ref, kt_ref, v1_ref):
    rows = kv_ref.shape[1]
    lane = lax.broadcasted_iota(jnp.int32, (rows, LANE), 1)
    kr = _rope_slab(jnp.where(lane < C_ROPE, kr_ref[0], 0.0), cos_ref[0], sin_ref[0])
    krt = jnp.transpose(kr).astype(kt_ref.dtype)
    ones = jnp.ones((rows, LANE), v1_ref.dtype)
    for h in range(C_HEADS):
        c0 = h * (C_NOPE + C_V)
        kt_ref[0, h, 0:C_NOPE, :] = jnp.transpose(kv_ref[0, :, c0:c0 + C_NOPE]).astype(kt_ref.dtype)
        kt_ref[0, h, C_NOPE:C_QK_PAD, :] = krt
        v1_ref[0, h, :, 0:C_V] = kv_ref[0, :, c0 + C_NOPE:c0 + C_NOPE + C_V].astype(v1_ref.dtype)
        v1_ref[0, h, :, C_V:C_V + LANE] = ones


def _kv_prep(kv, dkv_small, cos, sin, batch, seq):
    tm = _pick(seq, 256, LANE)
    r3 = lambda t: t.reshape(batch, seq, t.shape[-1])
    small_blk = C_KV_LORA // LANE
    width = C_HEADS * (C_NOPE + C_V)
    return pl.pallas_call(
        _kv_prep_kernel,
        grid=(batch, seq // tm),
        in_specs=[pl.BlockSpec((1, tm, width), lambda b, i: (b, i, 0)),
                  pl.BlockSpec((1, tm, LANE), lambda b, i: (b, i, small_blk)),
                  pl.BlockSpec((1, tm, LANE), lambda b, i: (b, i, 0)),
                  pl.BlockSpec((1, tm, LANE), lambda b, i: (b, i, 0))],
        out_specs=[pl.BlockSpec((1, C_HEADS, C_QK_PAD, tm), lambda b, i: (b, 0, 0, i)),
                   pl.BlockSpec((1, C_HEADS, tm, C_V + LANE), lambda b, i: (b, 0, i, 0))],
        out_shape=[jax.ShapeDtypeStruct((batch, C_HEADS, C_QK_PAD, seq), BF16),
                   jax.ShapeDtypeStruct((batch, C_HEADS, seq, C_V + LANE), BF16)],
        compiler_params=_params(("parallel", "parallel"), 2 * tm * width * 8 + 2 * tm * width * 4),
        name="mla_kv_prep",
    )(r3(kv), r3(dkv_small), r3(cos), r3(sin))


def _attn_kernel(q_ref, kt_ref, v1_ref, o_ref, s_ref, m_ref, acc_ref, *, tkv, nsub):
    tq = q_ref.shape[2]
    ts = tq // nsub
    nkv = kt_ref.shape[3] // tkv
    rep = tkv // LANE

    def scores_into(slot, blk):
        k0 = pl.multiple_of(blk * tkv, tkv)
        kt = kt_ref[0, 0, :, pl.ds(k0, tkv)]
        for s in range(nsub):
            s_ref[slot, s] = _dot(q_ref[0, 0, s * ts:(s + 1) * ts, :], kt)

    def consume(slot, blk):
        k0 = pl.multiple_of(blk * tkv, tkv)
        v1 = v1_ref[0, 0, pl.ds(k0, tkv), :]
        for s in range(nsub):
            sc = s_ref[slot, s]
            m = m_ref[s]
            m_new = jnp.maximum(m, jnp.max(sc, axis=1, keepdims=True))
            alpha = jnp.exp2(m - m_new)
            p = jnp.exp2(sc - jnp.tile(m_new, (1, rep))).astype(BF16)
            acc_ref[s] = jnp.tile(alpha, (1, 2)) * acc_ref[s] + _dot(p, v1)
            m_ref[s] = m_new

    m_ref[...] = jnp.full(m_ref.shape, -jnp.inf, F32)
    acc_ref[...] = jnp.zeros(acc_ref.shape, F32)
    scores_into(0, 0)

    def pair(jj, carry):
        b0 = 2 * jj
        scores_into(1, b0 + 1)
        consume(0, b0)
        scores_into(0, jnp.minimum(b0 + 2, nkv - 1))
        consume(1, b0 + 1)
        return carry

    lax.fori_loop(0, nkv // 2, pair, 0)
    if nkv % 2 == 1:
        consume(0, nkv - 1)
    for s in range(nsub):
        acc = acc_ref[s]
        o_ref[0, s * ts:(s + 1) * ts, :] = (acc[:, 0:C_V] / acc[:, C_V:C_V + LANE]).astype(o_ref.dtype)


def _attention(q, kt, v1):
    batch, heads, seq, _ = q.shape
    tq = _pick(seq, 1024, LANE)
    nsub = 2 if tq % (2 * LANE) == 0 else 1
    ts = tq // nsub
    tkv = _pick(seq, 1024, LANE)
    est = 2 * (tq * C_QK_PAD * 2 + 2 * seq * C_QK_PAD * 2 + tq * C_V * 2) + 8 * tkv * tq * 4
    return pl.pallas_call(
        functools.partial(_attn_kernel, tkv=tkv, nsub=nsub),
        grid=(batch, heads, seq // tq),
        in_specs=[pl.BlockSpec((1, 1, tq, C_QK_PAD), lambda b, h, i: (b, h, i, 0)),
                  pl.BlockSpec((1, 1, C_QK_PAD, seq), lambda b, h, i: (b, h, 0, 0)),
                  pl.BlockSpec((1, 1, seq, C_V + LANE), lambda b, h, i: (b, h, 0, 0))],
        out_specs=pl.BlockSpec((1, tq, C_V), lambda b, h, i: (b, i, h)),
        out_shape=jax.ShapeDtypeStruct((batch, seq, heads * C_V), BF16),
        scratch_shapes=[pltpu.VMEM((2, nsub, ts, tkv), F32), pltpu.VMEM((nsub, ts, LANE), F32),
                        pltpu.VMEM((nsub, ts, C_V + LANE), F32)],
        compiler_params=_params(("parallel", "parallel", "parallel"), est),
        name="mla_attention",
    )(q, kt, v1)


def _prepare_weights(p):
    bf = lambda t: t.astype(BF16)
    w_in = p["w_in"]
    c0 = 2 * A_WIDTH + 4 * B_WIDTH
    ab = w_in[:, :, c0:c0 + 4 * B_HEADS]
    dq = w_in[:, :, c0 + 4 * B_HEADS:c0 + 4 * B_HEADS + C_Q_LORA]
    dkv_kr = w_in[:, :, c0 + 4 * B_HEADS + C_Q_LORA:]
    depth = w_in.shape[0]
    uq = p["c_w_uq"].reshape(depth, C_Q_LORA, C_HEADS, C_NOPE + C_ROPE)
    uq = jnp.pad(uq, ((0, 0), (0, 0), (0, 0), (0, C_QK_PAD - C_NOPE - C_ROPE)))
    return dict(
        ffn1_up=bf(p["ffn1_up"]), ffn1_down=bf(p["ffn1_down"]), ffn2_up=bf(p["ffn2_up"]), ffn2_down=bf(p["ffn2_down"]),
        w_in=bf(w_in),
        w_dq=bf(dq),
        w_dkv_small=bf(jnp.concatenate([dkv_kr, ab], axis=2)),
        w_uq=bf(uq.reshape(depth, C_Q_LORA, C_HEADS * C_QK_PAD)),
        w_ukv=bf(p["c_w_ukv"]), w_gate=bf(p["w_gate"]), w_branch=bf(p["w_branch"]), w_o=bf(p["w_o"]),
        a_w_s=bf(p["a_w_s"]))


def kernel(x, positions, norm_ffn1, ffn1_up, ffn1_down, norm_mix, w_in, w_gate, a_v_gain, a_w_s, a_b_s, b_conv,
           b_a_log, b_dt_bias, b_o_gain, c_q_gain, c_kv_gain, c_w_uq, c_w_ukv, w_branch, w_o, norm_ffn2, ffn2_up,
           ffn2_down, norm_final):
    batch, seq, d = x.shape
    depth = norm_ffn1.shape[0]
    m = batch * seq
    w = _prepare_weights(dict(ffn1_up=ffn1_up, ffn1_down=ffn1_down, ffn2_up=ffn2_up, ffn2_down=ffn2_down, w_in=w_in,
                              w_gate=w_gate, c_w_uq=c_w_uq, c_w_ukv=c_w_ukv, w_branch=w_branch, w_o=w_o, a_w_s=a_w_s))

    cos, sin = _rope_tables(positions)
    xs = x.reshape(m, d)
    for l in range(depth):
        hn = _rmsnorm(xs, norm_ffn1[l], BF16)
        xs = _mm_res(_ffn_up(hn, w["ffn1_up"], l), w["ffn1_down"], l, xs, 0.5, name="ffn_down")

        h = _rmsnorm(xs, norm_mix[l], BF16)
        guv = _mm(h, w["w_in"], l, 0, 2 * A_WIDTH, F32, gelu=True, name="proj_a")
        qkvz = _mm(h, w["w_in"], l, 2 * A_WIDTH, 4 * B_WIDTH, F32, name="proj_b")
        dq = _mm(h, w["w_dq"], l, 0, C_Q_LORA, F32, name="proj_dq")
        dkv_small = _mm(h, w["w_dkv_small"], l, 0, C_KV_LORA + LANE, F32, name="proj_dkv")

        y_a = _gmlp(guv, a_v_gain[l], w["a_w_s"], l, a_b_s[l])

        gc, beta = _gdn_gate(dkv_small, b_a_log[l], b_dt_bias[l])
        qkvn = _conv_qkv(qkvz, b_conv[l], batch, seq)
        nh = 2 * B_HEADS
        o_f, o_b = _gdn(qkvn, gc.reshape(batch, seq, nh), beta.reshape(batch, seq, nh))
        y_b = _gdn_out(o_f.reshape(m, B_WIDTH), o_b.reshape(m, B_WIDTH), qkvz, b_o_gain[l])

        q = _norm_mm(dq, c_q_gain[l], w["w_uq"], l, F32, name="mla_uq")
        kv = _norm_mm(dkv_small, c_kv_gain[l], w["w_ukv"], l, F32, name="mla_ukv")
        qh = _q_prep(q, cos, sin, batch, seq)
        kt, v1 = _kv_prep(kv, dkv_small, cos, sin, batch, seq)
        y_c = _attention(qh, kt, v1).reshape(m, C_WIDTH)

        merged = _merge(h, jnp.stack([y_a, y_b, y_c], axis=0), w["w_gate"], w["w_branch"], l)
        xs = _mm_res(merged, w["w_o"], l, xs, 1.0, name="out_proj")

        hn = _rmsnorm(xs, norm_ffn2[l], BF16)
        xs = _mm_res(_ffn_up(hn, w["ffn2_up"], l), w["ffn2_down"], l, xs, 0.5, name="ffn_down")

    return _rmsnorm(xs, norm_final, F32).reshape(batch, seq, d)
```

```python
import functools
import math

import jax
import jax.numpy as jnp
from jax import lax
from jax.experimental import pallas as pl
from jax.experimental.pallas import tpu as pltpu

F32 = jnp.float32
BF16 = jnp.bfloat16
EPS = 1e-6
ROPE_THETA = 10000.0

A_GROUPS, A_GROUP_DIM, A_CHUNK = 16, 128, 128
A_WIDTH = A_GROUPS * A_GROUP_DIM
B_HEADS, B_HEAD_DIM, B_CONV, B_CHUNK = 16, 128, 5, 64
B_WIDTH = B_HEADS * B_HEAD_DIM
B_GROUP = 256
C_HEADS, C_Q_LORA, C_KV_LORA, C_NOPE, C_ROPE, C_V = 16, 1024, 512, 128, 64, 128
C_WIDTH = C_HEADS * C_V
C_QK_PAD = 256
N_BRANCH = 3

VMEM_BUDGET_BYTES = 44 * 1024 * 1024
VMEM_LIMIT_CAP_BYTES = 56 * 1024 * 1024
LANE = 128
SUBLANE = 8
HIGHEST = lax.Precision.HIGHEST


def _pick(n, pref, align):
    if n <= pref:
        return n
    t = (pref // align) * align
    while t >= align:
        if n % t == 0:
            return t
        t -= align
    return n


def _params(sem, vmem_bytes):
    limit = int(min(VMEM_LIMIT_CAP_BYTES, max(vmem_bytes + (8 << 20), 32 << 20)))
    return pltpu.CompilerParams(dimension_semantics=sem, vmem_limit_bytes=limit)


def _dot(a, b):
    return jnp.dot(a, b, preferred_element_type=F32)


def _dot_f32(a, b):
    return jnp.dot(a, b, preferred_element_type=F32, precision=HIGHEST)


def _rmsnorm_kernel(x_ref, g_ref, o_ref):
    x = x_ref[...]
    ms = jnp.mean(x * x, axis=-1, keepdims=True)
    o_ref[...] = (x * lax.rsqrt(ms + EPS) * g_ref[...]).astype(o_ref.dtype)


def _rmsnorm(x, gain, out_dtype):
    m, d = x.shape
    tm = _pick(m, 512, SUBLANE)
    return pl.pallas_call(
        _rmsnorm_kernel,
        grid=(m // tm,),
        in_specs=[pl.BlockSpec((tm, d), lambda i: (i, 0)), pl.BlockSpec((1, d), lambda i: (0, 0))],
        out_specs=pl.BlockSpec((tm, d), lambda i: (i, 0)),
        out_shape=jax.ShapeDtypeStruct((m, d), out_dtype),
        compiler_params=_params(("parallel",), 4 * tm * d * 6),
        name="rmsnorm",
    )(x, gain.reshape(1, d))


def _mm_kernel(a_ref, w_ref, o_ref, *, gelu):
    y = _dot(a_ref[...], w_ref[0])
    if gelu:
        y = 0.5 * y * (1.0 + lax.erf(y * math.sqrt(0.5)))
    o_ref[...] = y.astype(o_ref.dtype)


def _mm(a, w, layer, col0, n, out_dtype, gelu=False, name="mm"):
    m, k = a.shape
    osz = jnp.dtype(out_dtype).itemsize
    tm = _pick(m, 1024, SUBLANE)
    tn = _pick(math.gcd(n, col0) if col0 else n, 1024, LANE)
    est = lambda tn_: 2 * (tm * k * 2 + k * tn_ * 2 + tm * tn_ * osz) + tm * tn_ * 4
    while est(tn) > VMEM_BUDGET_BYTES and tn % (2 * LANE) == 0:
        tn //= 2
    jb = col0 // tn
    return pl.pallas_call(
        functools.partial(_mm_kernel, gelu=gelu),
        grid=(m // tm, n // tn),
        in_specs=[pl.BlockSpec((tm, k), lambda i, j: (i, 0)), pl.BlockSpec((1, k, tn), lambda i, j: (layer, 0, jb + j))],
        out_specs=pl.BlockSpec((tm, tn), lambda i, j: (i, j)),
        out_shape=jax.ShapeDtypeStruct((m, n), out_dtype),
        compiler_params=_params(("parallel", "parallel"), est(tn)),
        name=name,
    )(a, w)


def _norm_mm_kernel(a_ref, g_ref, w_ref, o_ref):
    x = a_ref[...]
    ms = jnp.mean(x * x, axis=-1, keepdims=True)
    xn = (x * lax.rsqrt(ms + EPS) * g_ref[...]).astype(BF16)
    o_ref[...] = _dot(xn, w_ref[0]).astype(o_ref.dtype)


def _norm_mm(a, gain, w, layer, out_dtype, name="norm_mm"):
    m = a.shape[0]
    _, k, n = w.shape
    tm = _pick(m, 1024, SUBLANE)
    tn = _pick(n, 1024, LANE)
    est = 2 * (tm * k * 4 + k * tn * 2 + tm * tn * 4) + tm * k * 6 + tm * tn * 4
    return pl.pallas_call(
        _norm_mm_kernel,
        grid=(m // tm, n // tn),
        in_specs=[pl.BlockSpec((tm, k), lambda i, j: (i, 0)), pl.BlockSpec((1, k), lambda i, j: (0, 0)),
                  pl.BlockSpec((1, k, tn), lambda i, j: (layer, 0, j))],
        out_specs=pl.BlockSpec((tm, tn), lambda i, j: (i, j)),
        out_shape=jax.ShapeDtypeStruct((m, n), out_dtype),
        compiler_params=_params(("parallel", "parallel"), est),
        name=name,
    )(a, gain.reshape(1, k), w)


def _ffn_up_kernel(a_ref, wg_ref, wu_ref, o_ref):
    a = a_ref[...]
    g = _dot(a, wg_ref[0])
    u = _dot(a, wu_ref[0])
    o_ref[...] = (g * jax.nn.sigmoid(g) * u).astype(o_ref.dtype)


def _ffn_up(a, w_up, layer):
    m, k = a.shape
    n = w_up.shape[2] // 2
    tn = _pick(n, 512, LANE)
    nj = n // tn
    tm = _pick(m, 2048 if tn <= 256 else 1024, SUBLANE)
    est = 2 * (tm * k * 2 + 2 * k * tn * 2 + tm * tn * 2) + 3 * tm * tn * 4
    return pl.pallas_call(
        _ffn_up_kernel,
        grid=(m // tm, nj),
        in_specs=[pl.BlockSpec((tm, k), lambda i, j: (i, 0)), pl.BlockSpec((1, k, tn), lambda i, j: (layer, 0, j)),
                  pl.BlockSpec((1, k, tn), lambda i, j: (layer, 0, nj + j))],
        out_specs=pl.BlockSpec((tm, tn), lambda i, j: (i, j)),
        out_shape=jax.ShapeDtypeStruct((m, n), BF16),
        compiler_params=_params(("parallel", "parallel"), est),
        name="ffn_up",
    )(a, w_up, w_up)


def _mm_res_kernel(a_ref, w_ref, r_ref, o_ref, *, scale):
    o_ref[...] = r_ref[...] + scale * _dot(a_ref[...], w_ref[0])


def _mm_res(a, w, layer, res, scale, name="mm_res"):
    m, k = a.shape
    n = w.shape[2]
    tm = _pick(m, 1024, SUBLANE)
    a_bufs = 2 if 2 * tm * k * 2 <= VMEM_BUDGET_BYTES // 2 else 1
    tn = _pick(n, 1024, LANE)
    est = lambda tn_: a_bufs * tm * k * 2 + 2 * (k * tn_ * 2 + 2 * tm * tn_ * 4) + tm * tn_ * 4
    while est(tn) > VMEM_BUDGET_BYTES and tn % (2 * LANE) == 0:
        tn //= 2
    a_spec = pl.BlockSpec((tm, k), lambda i, j: (i, 0), pipeline_mode=pl.Buffered(a_bufs))
    return pl.pallas_call(
        functools.partial(_mm_res_kernel, scale=scale),
        grid=(m // tm, n // tn),
        in_specs=[a_spec, pl.BlockSpec((1, k, tn), lambda i, j: (layer, 0, j)),
                  pl.BlockSpec((tm, tn), lambda i, j: (i, j))],
        out_specs=pl.BlockSpec((tm, tn), lambda i, j: (i, j)),
        out_shape=jax.ShapeDtypeStruct((m, n), F32),
        compiler_params=_params(("parallel", "arbitrary"), est(tn)),
        name=name,
    )(a, w, res)


def _merge_kernel(h_ref, y_ref, wg_ref, wb_ref, o_ref, acc_ref):
    br = pl.program_id(2)
    gate = jax.nn.sigmoid(_dot(h_ref[...], wg_ref[0]))
    term = gate * _dot(y_ref[0], wb_ref[0, 0])

    @pl.when(br == 0)
    def _():
        acc_ref[...] = term

    @pl.when(br == 1)
    def _():
        acc_ref[...] += term

    @pl.when(br == N_BRANCH - 1)
    def _():
        o_ref[...] = (acc_ref[...] + term).astype(o_ref.dtype)


def _merge(h, ys, w_gate, w_branch, layer):
    m, d = h.shape
    wdt = ys.shape[2]
    tm = _pick(m, 1024, SUBLANE)
    tn = _pick(d, 512, LANE)
    nj = d // tn
    est = 2 * (tm * d * 2 + tm * wdt * 2 + d * tn * 2 + wdt * tn * 2 + tm * tn * 2) + 4 * tm * tn * 4
    return pl.pallas_call(
        _merge_kernel,
        grid=(m // tm, nj, N_BRANCH),
        in_specs=[pl.BlockSpec((tm, d), lambda i, j, b: (i, 0)),
                  pl.BlockSpec((1, tm, wdt), lambda i, j, b: (b, i, 0)),
                  pl.BlockSpec((1, d, tn), lambda i, j, b: (layer, 0, b * nj + j)),
                  pl.BlockSpec((1, 1, wdt, tn), lambda i, j, b: (layer, b, 0, j))],
        out_specs=pl.BlockSpec((tm, tn), lambda i, j, b: (i, j)),
        out_shape=jax.ShapeDtypeStruct((m, d), BF16),
        scratch_shapes=[pltpu.VMEM((tm, tn), F32)],
        compiler_params=_params(("parallel", "parallel", "arbitrary"), est),
        name="merge",
    )(h, ys, w_gate, w_branch)


def _gmlp_kernel(u_ref, v_ref, g_ref, ws_ref, bt_ref, o_ref):
    v = v_ref[...]
    ms = jnp.mean(v * v, axis=-1, keepdims=True)
    vn = (v * lax.rsqrt(ms + EPS) * g_ref[...]).astype(BF16)
    rows = v.shape[0]
    for c in range(rows // A_CHUNK):
        r0 = c * A_CHUNK
        for g in range(A_GROUPS):
            c0 = g * A_GROUP_DIM
            s = _dot(ws_ref[0, g], vn[r0:r0 + A_CHUNK, c0:c0 + A_GROUP_DIM]) + bt_ref[:, g:g + 1]
            o_ref[r0:r0 + A_CHUNK, c0:c0 + A_GROUP_DIM] = (
                u_ref[r0:r0 + A_CHUNK, c0:c0 + A_GROUP_DIM] * s).astype(o_ref.dtype)


def _gmlp(guv, v_gain, w_s, layer, b_s):
    m = guv.shape[0]
    tm = _pick(m, 256, A_CHUNK)
    est = 2 * (2 * tm * A_WIDTH * 4 + tm * A_WIDTH * 2) + 3 * tm * A_WIDTH * 4
    return pl.pallas_call(
        _gmlp_kernel,
        grid=(m // tm,),
        in_specs=[pl.BlockSpec((tm, A_WIDTH), lambda i: (i, 0)), pl.BlockSpec((tm, A_WIDTH), lambda i: (i, 1)),
                  pl.BlockSpec((1, A_WIDTH), lambda i: (0, 0)),
                  pl.BlockSpec((1, A_GROUPS, A_CHUNK, A_CHUNK), lambda i: (layer, 0, 0, 0)),
                  pl.BlockSpec((A_CHUNK, A_GROUPS), lambda i: (0, 0))],
        out_specs=pl.BlockSpec((tm, A_WIDTH), lambda i: (i, 0)),
        out_shape=jax.ShapeDtypeStruct((m, A_WIDTH), BF16),
        compiler_params=_params(("parallel",), est),
        name="gmlp",
    )(guv, guv, v_gain.reshape(1, A_WIDTH), w_s, b_s.T)


def _gdn_gate_kernel(s_ref, alog_ref, dtb_ref, gc_ref, beta_ref):
    x = s_ref[...]
    a = x[:, C_ROPE:C_ROPE + 2 * B_HEADS]
    b = x[:, C_ROPE + 2 * B_HEADS:C_ROPE + 4 * B_HEADS]
    z = a + dtb_ref[...]
    softplus = jnp.maximum(z, 0.0) + jnp.log1p(jnp.exp(-jnp.abs(z)))
    g = -jnp.exp(alog_ref[...]) * softplus
    beta_ref[...] = jax.nn.sigmoid(b)
    ri = lax.broadcasted_iota(jnp.int32, (B_CHUNK, B_CHUNK), 0)
    ci = lax.broadcasted_iota(jnp.int32, (B_CHUNK, B_CHUNK), 1)
    tril = (ri >= ci).astype(F32)
    triu = (ri <= ci).astype(F32)
    fwd = lax.broadcasted_iota(jnp.int32, (B_CHUNK, 2 * B_HEADS), 1) < B_HEADS
    for c in range(x.shape[0] // B_CHUNK):
        gcnk = g[c * B_CHUNK:(c + 1) * B_CHUNK]
        gc_ref[c * B_CHUNK:(c + 1) * B_CHUNK, :] = jnp.where(fwd, _dot_f32(tril, gcnk), _dot_f32(triu, gcnk))


def _gdn_gate(dkv_small, a_log, dt_bias):
    m = dkv_small.shape[0]
    tm = _pick(m, 512, B_CHUNK)
    nh = 2 * B_HEADS
    small_blk = C_KV_LORA // LANE
    return pl.pallas_call(
        _gdn_gate_kernel,
        grid=(m // tm,),
        in_specs=[pl.BlockSpec((tm, LANE), lambda i: (i, small_blk)), pl.BlockSpec((1, nh), lambda i: (0, 0)),
                  pl.BlockSpec((1, nh), lambda i: (0, 0))],
        out_specs=[pl.BlockSpec((tm, nh), lambda i: (i, 0)), pl.BlockSpec((tm, nh), lambda i: (i, 0))],
        out_shape=[jax.ShapeDtypeStruct((m, nh), F32), jax.ShapeDtypeStruct((m, nh), F32)],
        compiler_params=_params(("parallel",), 1 << 20),
        name="gdn_gate",
    )(dkv_small, a_log.reshape(1, nh), dt_bias.reshape(1, nh))


def _conv_kernel(x_ref, p_ref, n_ref, w_ref, o_ref, xe_ref, *, ts, nblk):
    i = pl.program_id(1)
    j = pl.program_id(2)
    halo = SUBLANE
    xe_ref[0:halo, :] = jnp.where(i > 0, p_ref[0], 0.0)
    xe_ref[halo:halo + ts, :] = x_ref[0]
    xe_ref[halo + ts:halo + ts + halo, :] = jnp.where(i < nblk - 1, n_ref[0], 0.0)
    pad = (B_CONV - 1) // 2
    acc = None
    for t in range(B_CONV):
        term = w_ref[t:t + 1, :] * xe_ref[pl.ds(halo - pad + t, ts), :]
        acc = term if acc is None else acc + term
    y = acc * jax.nn.sigmoid(acc)
    qscale = jnp.where(j == 0, B_HEAD_DIM ** -0.5, 1.0).astype(F32)
    for h in range(B_HEADS):
        c0 = h * B_HEAD_DIM
        yh = y[:, c0:c0 + B_HEAD_DIM]
        ss = jnp.sum(yh * yh, axis=-1, keepdims=True)
        nrm = jnp.where(j < 2, lax.rsqrt(ss + EPS) * qscale, 1.0)
        o_ref[0, :, c0:c0 + B_HEAD_DIM] = yh * nrm


def _conv_qkv(qkvz, conv_w, batch, seq):
    x = qkvz.reshape(batch, seq, 4 * B_WIDTH)
    ts = _pick(seq, 256, SUBLANE)
    nblk = seq // ts
    hb = ts // SUBLANE
    nh8 = seq // SUBLANE
    tc = B_WIDTH
    est = 2 * (2 * ts * tc * 4 + 2 * SUBLANE * tc * 4) + 4 * (ts + 16) * tc * 4
    return pl.pallas_call(
        functools.partial(_conv_kernel, ts=ts, nblk=nblk),
        grid=(batch, nblk, 3),
        in_specs=[pl.BlockSpec((1, ts, tc), lambda b, i, j: (b, i, j)),
                  pl.BlockSpec((1, SUBLANE, tc), lambda b, i, j: (b, jnp.maximum(i * hb - 1, 0), j)),
                  pl.BlockSpec((1, SUBLANE, tc), lambda b, i, j: (b, jnp.minimum((i + 1) * hb, nh8 - 1), j)),
                  pl.BlockSpec((B_CONV, tc), lambda b, i, j: (0, j))],
        out_specs=pl.BlockSpec((1, ts, tc), lambda b, i, j: (b, i, j)),
        out_shape=jax.ShapeDtypeStruct((batch, seq, 3 * B_WIDTH), F32),
        scratch_shapes=[pltpu.VMEM((ts + 2 * SUBLANE, tc), F32)],
        compiler_params=_params(("parallel", "parallel", "parallel"), est),
        name="gdn_conv",
    )(x, x, x, conv_w)


def _gdn_kernel(qf_ref, kf_ref, vf_ref, gf_ref, bf_ref, gtf_ref,
                qb_ref, kb_ref, vb_ref, gb_ref, bb_ref, gtb_ref,
                of_ref, ob_ref, sf_ref, sb_ref, *, ngroup):
    h = pl.program_id(1)
    i = pl.program_id(2)

    @pl.when(i == 0)
    def _():
        sf_ref[...] = jnp.zeros_like(sf_ref)
        sb_ref[...] = jnp.zeros_like(sb_ref)

    g_rows, c_rows, dk = B_GROUP, B_CHUNK, B_HEAD_DIM
    cpg = g_rows // c_rows
    rid = lax.broadcasted_iota(jnp.int32, (g_rows, g_rows), 0)
    cid = lax.broadcasted_iota(jnp.int32, (g_rows, g_rows), 1)
    same_chunk = (rid // c_rows) == (cid // c_rows)
    eye = (rid == cid).astype(F32)
    lane = lax.broadcasted_iota(jnp.int32, (1, 2 * B_HEADS), 1)
    dirs = (
        dict(q=qf_ref, k=kf_ref, v=vf_ref, g=gf_ref, b=bf_ref, gt=gtf_ref, o=of_ref, s=sf_ref,
             col=h, incl=jnp.logical_and(same_chunk, rid >= cid), strict=jnp.logical_and(same_chunk, rid > cid),
             last=c_rows - 1, rev=False),
        dict(q=qb_ref, k=kb_ref, v=vb_ref, g=gb_ref, b=bb_ref, gt=gtb_ref, o=ob_ref, s=sb_ref,
             col=h + B_HEADS, incl=jnp.logical_and(same_chunk, rid <= cid),
             strict=jnp.logical_and(same_chunk, rid < cid), last=0, rev=True),
    )

    units = []
    for d in dirs:
        sel = lane == d["col"]
        grow_all = d["gt"][0, pl.ds(d["col"], 1), :]
        order = range(ngroup - 1, -1, -1) if d["rev"] else range(ngroup)
        for gi in order:
            r0 = gi * g_rows
            q = d["q"][0, r0:r0 + g_rows, :]
            k = d["k"][0, r0:r0 + g_rows, :]
            v = d["v"][0, r0:r0 + g_rows, :]
            gcol = jnp.sum(jnp.where(sel, d["g"][0, r0:r0 + g_rows, :], 0.0), axis=-1, keepdims=True)
            bcol = jnp.sum(jnp.where(sel, d["b"][0, r0:r0 + g_rows, :], 0.0), axis=-1, keepdims=True)
            grow = grow_all[:, r0:r0 + g_rows]
            eg = jnp.exp(gcol)
            kbeta = k * bcol
            decay = jnp.where(d["incl"], jnp.exp(jnp.where(d["incl"], gcol - grow, 0.0)), 0.0)
            x = lax.dot_general(jnp.concatenate([kbeta, q], axis=0).astype(BF16), k.astype(BF16),
                                (((1,), (1,)), ((), ())), preferred_element_type=F32)
            n = -jnp.where(d["strict"], x[0:g_rows] * decay, 0.0)
            units.append(dict(d=d, r0=r0, q=q, k=k, eg=eg, gcol=gcol, n=n, attn=x[g_rows:] * decay,
                              rhs=jnp.concatenate([v * bcol, kbeta * eg], axis=1)))

    for u in units:
        u["nb"] = u["n"].astype(BF16)
        u["p"] = eye + u["n"]
    for _ in range(int(math.log2(c_rows)) - 1):
        for u in units:
            u["nb"] = _dot(u["nb"], u["nb"]).astype(BF16)
        for u in units:
            u["p"] = u["p"] + _dot(u["p"].astype(BF16), u["nb"])

    for u in units:
        u["uw"] = _dot(u["p"].astype(BF16), u["rhs"].astype(BF16)).astype(BF16)
    for u in units:
        au = _dot(u["attn"].astype(BF16), u["uw"])
        u["ob"] = au[:, 0:dk]
        u["qeff"] = u["q"] * u["eg"] - au[:, dk:2 * dk]

    steps = {False: [], True: []}
    for u in units:
        d = u["d"]
        corder = range(cpg - 1, -1, -1) if d["rev"] else range(cpg)
        for c in corder:
            c0 = c * c_rows
            gcolc = u["gcol"][c0:c0 + c_rows]
            glc = gcolc[d["last"]:d["last"] + 1, :]
            k_dec = (u["k"][c0:c0 + c_rows] * jnp.exp(glc - gcolc)).astype(BF16)
            kuw = lax.dot_general(k_dec, u["uw"][c0:c0 + c_rows, :], (((0,), (0,)), ((), ())),
                                  preferred_element_type=F32)
            lhs = jnp.concatenate([-kuw[:, dk:2 * dk], u["qeff"][c0:c0 + c_rows]], axis=0).astype(BF16)
            steps[d["rev"]].append(dict(lhs=lhs, bmat=kuw[:, 0:dk], egl=jnp.exp(glc),
                                        ob=u["ob"][c0:c0 + c_rows], row=u["r0"] + c0))

    state = {False: sf_ref[...], True: sb_ref[...]}
    for t in range(ngroup * cpg):
        for d in dirs:
            st = steps[d["rev"]][t]
            ys = _dot(st["lhs"], state[d["rev"]].astype(BF16))
            state[d["rev"]] = state[d["rev"]] * st["egl"] + ys[0:dk] + st["bmat"]
            d["o"][0, st["row"]:st["row"] + c_rows, :] = ys[dk:dk + c_rows] + st["ob"]
    sf_ref[...] = state[False]
    sb_ref[...] = state[True]


def _gdn(qkvn, gc, beta):
    batch, seq, _ = qkvn.shape
    rows = _pick(seq, 1024, B_GROUP)
    nb = seq // rows
    nh = 2 * B_HEADS
    hd = B_HEAD_DIM
    gc_t = jnp.transpose(gc, (0, 2, 1))

    def blk(i, rev):
        return nb - 1 - i if rev else i

    in_specs = []
    for rev in (False, True):
        col = lambda c0, rev=rev: pl.BlockSpec((1, rows, hd), lambda b, h, i: (b, blk(i, rev), c0 + h))
        gsp = pl.BlockSpec((1, rows, nh), lambda b, h, i, rev=rev: (b, blk(i, rev), 0))
        in_specs += [col(0), col(B_HEADS), col(2 * B_HEADS), gsp, gsp,
                     pl.BlockSpec((1, nh, rows), lambda b, h, i, rev=rev: (b, 0, blk(i, rev)))]
    out_specs = [pl.BlockSpec((1, rows, hd), lambda b, h, i: (b, i, h)),
                 pl.BlockSpec((1, rows, hd), lambda b, h, i: (b, nb - 1 - i, h))]
    out_sd = jax.ShapeDtypeStruct((batch, seq, B_WIDTH), F32)
    args = (qkvn, qkvn, qkvn, gc, beta, gc_t)
    return pl.pallas_call(
        functools.partial(_gdn_kernel, ngroup=rows // B_GROUP),
        grid=(batch, B_HEADS, nb),
        in_specs=in_specs,
        out_specs=out_specs,
        out_shape=[out_sd, out_sd],
        scratch_shapes=[pltpu.VMEM((hd, hd), F32), pltpu.VMEM((hd, hd), F32)],
        compiler_params=_params(("parallel", "parallel", "arbitrary"), 24 << 20),
        name="gdn_scan",
    )(*args, *args)


def _gdn_out_kernel(of_ref, ob_ref, z_ref, g_ref, o_ref):
    for h in range(B_HEADS):
        c0 = h * B_HEAD_DIM
        o = of_ref[:, c0:c0 + B_HEAD_DIM] + ob_ref[:, c0:c0 + B_HEAD_DIM]
        ms = jnp.mean(o * o, axis=-1, keepdims=True)
        z = z_ref[:, c0:c0 + B_HEAD_DIM]
        o_ref[:, c0:c0 + B_HEAD_DIM] = (o * lax.rsqrt(ms + EPS) * g_ref[...] * (z * jax.nn.sigmoid(z))).astype(o_ref.dtype)


def _gdn_out(o_f, o_b, qkvz, o_gain):
    m = o_f.shape[0]
    tm = _pick(m, 512, SUBLANE)
    blk = lambda j: pl.BlockSpec((tm, B_WIDTH), lambda i: (i, j))
    return pl.pallas_call(
        _gdn_out_kernel,
        grid=(m // tm,),
        in_specs=[blk(0), blk(0), blk(3), pl.BlockSpec((1, B_HEAD_DIM), lambda i: (0, 0))],
        out_specs=blk(0),
        out_shape=jax.ShapeDtypeStruct((m, B_WIDTH), BF16),
        compiler_params=_params(("parallel",), 2 * tm * B_WIDTH * 14 + 4 * tm * B_WIDTH * 4),
        name="gdn_out",
    )(o_f, o_b, qkvz, o_gain.reshape(1, B_HEAD_DIM))


def _rope_tables_kernel(pos_ref, f_ref, cos_ref, sin_ref):
    ang = pos_ref[...].astype(F32) * f_ref[...]
    cos_ref[...] = jnp.cos(ang)
    sin_ref[...] = jnp.sin(ang)


def _rope_tables(positions):
    m = positions.size
    inv_freq = ROPE_THETA ** (-jnp.arange(0, C_ROPE, 2, dtype=F32) / C_ROPE)
    f = jnp.tile(inv_freq, LANE // (C_ROPE // 2)).reshape(1, LANE)
    tm = _pick(m, 1024, SUBLANE)
    sd = jax.ShapeDtypeStruct((m, LANE), F32)
    return pl.pallas_call(
        _rope_tables_kernel,
        grid=(m // tm,),
        in_specs=[pl.BlockSpec((tm, 1), lambda i: (i, 0)), pl.BlockSpec((1, LANE), lambda i: (0, 0))],
        out_specs=[pl.BlockSpec((tm, LANE), lambda i: (i, 0))] * 2,
        out_shape=[sd, sd],
        compiler_params=_params(("parallel",), 4 << 20),
        name="rope_tables",
    )(positions.reshape(m, 1), f)


def _rope_slab(t, cos, sin):
    half = C_ROPE // 2
    lane = lax.broadcasted_iota(jnp.int32, t.shape, 1)
    lo = lane < half
    mid = jnp.logical_and(lane >= half, lane < C_ROPE)
    t2_to_lo = pltpu.roll(t, LANE - half, 1)
    t1_to_mid = pltpu.roll(t, half, 1)
    return jnp.where(lo, t * cos - t2_to_lo * sin, jnp.where(mid, t1_to_mid * sin + t * cos, 0.0))


def _q_prep_kernel(q_ref, cos_ref, sin_ref, o_ref, *, scale):
    cos, sin = cos_ref[0], sin_ref[0]
    for h in range(C_HEADS):
        c0 = h * C_QK_PAD
        qr = _rope_slab(q_ref[0, :, c0 + C_NOPE:c0 + C_QK_PAD], cos, sin)
        o_ref[0, h, :, 0:C_NOPE] = (q_ref[0, :, c0:c0 + C_NOPE] * scale).astype(o_ref.dtype)
        o_ref[0, h, :, C_NOPE:C_QK_PAD] = (qr * scale).astype(o_ref.dtype)


def _q_prep(q, cos, sin, batch, seq):
    tm = _pick(seq, 256, SUBLANE)
    scale = (C_NOPE + C_ROPE) ** -0.5 * math.log2(math.e)
    width = C_HEADS * C_QK_PAD
    return pl.pallas_call(
        functools.partial(_q_prep_kernel, scale=scale),
        grid=(batch, seq // tm),
        in_specs=[pl.BlockSpec((1, tm, width), lambda b, i: (b, i, 0)),
                  pl.BlockSpec((1, tm, LANE), lambda b, i: (b, i, 0)),
                  pl.BlockSpec((1, tm, LANE), lambda b, i: (b, i, 0))],
        out_specs=pl.BlockSpec((1, C_HEADS, tm, C_QK_PAD), lambda b, i: (b, 0, i, 0)),
        out_shape=jax.ShapeDtypeStruct((batch, C_HEADS, seq, C_QK_PAD), BF16),
        compiler_params=_params(("parallel", "parallel"), 2 * tm * width * 6 + 2 * tm * width * 4),
        name="mla_q_prep",
    )(q.reshape(batch, seq, width), cos.reshape(batch, seq, LANE), sin.reshape(batch, seq, LANE))


def _kv_prep_kernel(kv_ref, kr_ref, cos_ref, sin_ref, kt_ref, v1_ref):
    rows = kv_ref.shape[1]
    lane = lax.broadcasted_iota(jnp.int32, (rows, LANE), 1)
    kr = _rope_slab(jnp.where(lane < C_ROPE, kr_ref[0], 0.0), cos_ref[0], sin_ref[0])
    krt = jnp.transpose(kr).astype(kt_ref.dtype)
    ones = jnp.ones((rows, LANE), v1_ref.dtype)
    for h in range(C_HEADS):
        c0 = h * (C_NOPE + C_V)
        kt_ref[0, h, 0:C_NOPE, :] = jnp.transpose(kv_ref[0, :, c0:c0 + C_NOPE]).astype(kt_ref.dtype)
        kt_ref[0, h, C_NOPE:C_QK_PAD, :] = krt
        v1_ref[0, h, :, 0:C_V] = kv_ref[0, :, c0 + C_NOPE:c0 + C_NOPE + C_V].astype(v1_ref.dtype)
        v1_ref[0, h, :, C_V:C_V + LANE] = ones


def _kv_prep(kv, dkv_small, cos, sin, batch, seq):
    tm = _pick(seq, 256, LANE)
    r3 = lambda t: t.reshape(batch, seq, t.shape[-1])
    small_blk = C_KV_LORA // LANE
    width = C_HEADS * (C_NOPE + C_V)
    return pl.pallas_call(
        _kv_prep_kernel,
        grid=(batch, seq // tm),
        in_specs=[pl.BlockSpec((1, tm, width), lambda b, i: (b, i, 0)),
                  pl.BlockSpec((1, tm, LANE), lambda b, i: (b, i, small_blk)),
                  pl.BlockSpec((1, tm, LANE), lambda b, i: (b, i, 0)),
                  pl.BlockSpec((1, tm, LANE), lambda b, i: (b, i, 0))],
        out_specs=[pl.BlockSpec((1, C_HEADS, C_QK_PAD, tm), lambda b, i: (b, 0, 0, i)),
                   pl.BlockSpec((1, C_HEADS, tm, C_V + LANE), lambda b, i: (b, 0, i, 0))],
        out_shape=[jax.ShapeDtypeStruct((batch, C_HEADS, C_QK_PAD, seq), BF16),
                   jax.ShapeDtypeStruct((batch, C_HEADS, seq, C_V + LANE), BF16)],
        compiler_params=_params(("parallel", "parallel"), 2 * tm * width * 8 + 2 * tm * width * 4),
        name="mla_kv_prep",
    )(r3(kv), r3(dkv_small), r3(cos), r3(sin))


def _attn_kernel(q_ref, kt_ref, v1_ref, o_ref, s_ref, m_ref, acc_ref, *, tq, tkv, nsub):
    seq = q_ref.shape[2]
    ts = tq // nsub
    nq = seq // tq
    nkv = seq // tkv
    rep = tkv // LANE

    def scores_into(slot, q0, blk):
        k0 = pl.multiple_of(blk * tkv, tkv)
        kt = kt_ref[0, 0, :, pl.ds(k0, tkv)]
        for s in range(nsub):
            s_ref[slot, s] = _dot(q_ref[0, 0, pl.ds(pl.multiple_of(q0 + s * ts, ts), ts), :], kt)

    def consume(slot, blk):
        k0 = pl.multiple_of(blk * tkv, tkv)
        v1 = v1_ref[0, 0, pl.ds(k0, tkv), :]
        for s in range(nsub):
            sc = s_ref[slot, s]
            m = m_ref[s]
            m_new = jnp.maximum(m, jnp.max(sc, axis=1, keepdims=True))
            alpha = jnp.exp2(m - m_new)
            p = jnp.exp2(sc - jnp.tile(m_new, (1, rep))).astype(BF16)
            acc_ref[s] = jnp.tile(alpha, (1, 2)) * acc_ref[s] + _dot(p, v1)
            m_ref[s] = m_new

    def pair(q0, b0):
        scores_into(1, q0, b0 + 1)
        consume(0, b0)
        scores_into(0, q0, b0 + 2)
        consume(1, b0 + 1)

    def qtile(qi, carry):
        q0 = qi * tq
        q_next = jnp.minimum(qi + 1, nq - 1) * tq
        m_ref[...] = jnp.full(m_ref.shape, -jnp.inf, F32)
        acc_ref[...] = jnp.zeros(acc_ref.shape, F32)
        if nkv % 2 == 0:
            lax.fori_loop(0, nkv // 2 - 1, lambda jj, c: (pair(q0, 2 * jj), c)[1], 0)
            scores_into(1, q0, nkv - 1)
            consume(0, nkv - 2)
            scores_into(0, q_next, 0)
            consume(1, nkv - 1)
        else:
            lax.fori_loop(0, nkv // 2, lambda jj, c: (pair(q0, 2 * jj), c)[1], 0)
            consume(0, nkv - 1)
            scores_into(0, q_next, 0)
        for s in range(nsub):
            acc = acc_ref[s]
            o_ref[0, pl.ds(pl.multiple_of(q0 + s * ts, ts), ts), :] = (
                acc[:, 0:C_V] / acc[:, C_V:C_V + LANE]).astype(o_ref.dtype)
        return carry

    scores_into(0, 0, 0)
    lax.fori_loop(0, nq, qtile, 0)


def _attention(q, kt, v1):
    batch, heads, seq, _ = q.shape
    tq = _pick(seq, 1024, LANE)
    nsub = 2 if tq % (2 * LANE) == 0 else 1
    ts = tq // nsub
    tkv = _pick(seq, 1024, LANE)
    est = 2 * (3 * seq * C_QK_PAD * 2 + seq * C_V * 2) + 8 * tkv * tq * 4
    return pl.pallas_call(
        functools.partial(_attn_kernel, tq=tq, tkv=tkv, nsub=nsub),
        grid=(batch, heads),
        in_specs=[pl.BlockSpec((1, 1, seq, C_QK_PAD), lambda b, h: (b, h, 0, 0)),
                  pl.BlockSpec((1, 1, C_QK_PAD, seq), lambda b, h: (b, h, 0, 0)),
                  pl.BlockSpec((1, 1, seq, C_V + LANE), lambda b, h: (b, h, 0, 0))],
        out_specs=pl.BlockSpec((1, seq, C_V), lambda b, h: (b, 0, h)),
        out_shape=jax.ShapeDtypeStruct((batch, seq, heads * C_V), BF16),
        scratch_shapes=[pltpu.VMEM((2, nsub, ts, tkv), F32), pltpu.VMEM((nsub, ts, LANE), F32),
                        pltpu.VMEM((nsub, ts, C_V + LANE), F32)],
        compiler_params=_params(("parallel", "parallel"), est),
        name="mla_attention",
    )(q, kt, v1)


def _prepare_weights(p):
    bf = lambda t: t.astype(BF16)
    w_in = p["w_in"]
    c0 = 2 * A_WIDTH + 4 * B_WIDTH
    ab = w_in[:, :, c0:c0 + 4 * B_HEADS]
    dq = w_in[:, :, c0 + 4 * B_HEADS:c0 + 4 * B_HEADS + C_Q_LORA]
    dkv_kr = w_in[:, :, c0 + 4 * B_HEADS + C_Q_LORA:]
    depth = w_in.shape[0]
    uq = p["c_w_uq"].reshape(depth, C_Q_LORA, C_HEADS, C_NOPE + C_ROPE)
    uq = jnp.pad(uq, ((0, 0), (0, 0), (0, 0), (0, C_QK_PAD - C_NOPE - C_ROPE)))
    return dict(
        ffn1_up=bf(p["ffn1_up"]), ffn1_down=bf(p["ffn1_down"]), ffn2_up=bf(p["ffn2_up"]), ffn2_down=bf(p["ffn2_down"]),
        w_in=bf(w_in),
        w_dq=bf(dq),
        w_dkv_small=bf(jnp.concatenate([dkv_kr, ab], axis=2)),
        w_uq=bf(uq.reshape(depth, C_Q_LORA, C_HEADS * C_QK_PAD)),
        w_ukv=bf(p["c_w_ukv"]), w_gate=bf(p["w_gate"]), w_branch=bf(p["w_branch"]), w_o=bf(p["w_o"]),
        a_w_s=bf(p["a_w_s"]))


def kernel(x, positions, norm_ffn1, ffn1_up, ffn1_down, norm_mix, w_in, w_gate, a_v_gain, a_w_s, a_b_s, b_conv,
           b_a_log, b_dt_bias, b_o_gain, c_q_gain, c_kv_gain, c_w_uq, c_w_ukv, w_branch, w_o, norm_ffn2, ffn2_up,
           ffn2_down, norm_final):
    batch, seq, d = x.shape
    depth = norm_ffn1.shape[0]
    m = batch * seq
    w = _prepare_weights(dict(ffn1_up=ffn1_up, ffn1_down=ffn1_down, ffn2_up=ffn2_up, ffn2_down=ffn2_down, w_in=w_in,
                              w_gate=w_gate, c_w_uq=c_w_uq, c_w_ukv=c_w_ukv, w_branch=w_branch, w_o=w_o, a_w_s=a_w_s))

    cos, sin = _rope_tables(positions)
    xs = x.reshape(m, d)
    for l in range(depth):
        hn = _rmsnorm(xs, norm_ffn1[l], BF16)
        xs = _mm_res(_ffn_up(hn, w["ffn1_up"], l), w["ffn1_down"], l, xs, 0.5, name="ffn_down")

        h = _rmsnorm(xs, norm_mix[l], BF16)
        guv = _mm(h, w["w_in"], l, 0, 2 * A_WIDTH, F32, gelu=True, name="proj_a")
        qkvz = _mm(h, w["w_in"], l, 2 * A_WIDTH, 4 * B_WIDTH, F32, name="proj_b")
        dq = _mm(h, w["w_dq"], l, 0, C_Q_LORA, F32, name="proj_dq")
        dkv_small = _mm(h, w["w_dkv_small"], l, 0, C_KV_LORA + LANE, F32, name="proj_dkv")

        y_a = _gmlp(guv, a_v_gain[l], w["a_w_s"], l, a_b_s[l])

        gc, beta = _gdn_gate(dkv_small, b_a_log[l], b_dt_bias[l])
        qkvn = _conv_qkv(qkvz, b_conv[l], batch, seq)
        nh = 2 * B_HEADS
        o_f, o_b = _gdn(qkvn, gc.reshape(batch, seq, nh), beta.reshape(batch, seq, nh))
        y_b = _gdn_out(o_f.reshape(m, B_WIDTH), o_b.reshape(m, B_WIDTH), qkvz, b_o_gain[l])

        q = _norm_mm(dq, c_q_gain[l], w["w_uq"], l, F32, name="mla_uq")
        kv = _norm_mm(dkv_small, c_kv_gain[l], w["w_ukv"], l, F32, name="mla_ukv")
        qh = _q_prep(q, cos, sin, batch, seq)
        kt, v1 = _kv_prep(kv, dkv_small, cos, sin, batch, seq)
        y_c = _attention(qh, kt, v1).reshape(m, C_WIDTH)

        merged = _merge(h, jnp.stack([y_a, y_b, y_c], axis=0), w["w_gate"], w["w_branch"], l)
        xs = _mm_res(merged, w["w_o"], l, xs, 1.0, name="out_proj")

        hn = _rmsnorm(xs, norm_ffn2[l], BF16)
        xs = _mm_res(_ffn_up(hn, w["ffn2_up"], l), w["ffn2_down"], l, xs, 0.5, name="ffn_down")

    return _rmsnorm(xs, norm_final, F32).reshape(batch, seq, d)
```

```python
import functools
import math

import jax
import jax.numpy as jnp
from jax import lax
from jax.experimental import pallas as pl
from jax.experimental.pallas import tpu as pltpu

F32 = jnp.float32
BF16 = jnp.bfloat16
EPS = 1e-6
ROPE_THETA = 10000.0

A_GROUPS, A_GROUP_DIM, A_CHUNK = 16, 128, 128
A_WIDTH = A_GROUPS * A_GROUP_DIM
B_HEADS, B_HEAD_DIM, B_CONV, B_CHUNK = 16, 128, 5, 64
B_WIDTH = B_HEADS * B_HEAD_DIM
B_GROUP = 256
C_HEADS, C_Q_LORA, C_KV_LORA, C_NOPE, C_ROPE, C_V = 16, 1024, 512, 128, 64, 128
C_WIDTH = C_HEADS * C_V
C_QK_PAD = 256
N_BRANCH = 3

VMEM_BUDGET_BYTES = 44 * 1024 * 1024
VMEM_LIMIT_CAP_BYTES = 56 * 1024 * 1024
LANE = 128
SUBLANE = 8
HIGHEST = lax.Precision.HIGHEST


def _pick(n, pref, align):
    if n <= pref:
        return n
    t = (pref // align) * align
    while t >= align:
        if n % t == 0:
            return t
        t -= align
    return n


def _params(sem, vmem_bytes):
    limit = int(min(VMEM_LIMIT_CAP_BYTES, max(vmem_bytes + (8 << 20), 32 << 20)))
    return pltpu.CompilerParams(dimension_semantics=sem, vmem_limit_bytes=limit)


def _dot(a, b):
    return jnp.dot(a, b, preferred_element_type=F32)


def _dot_f32(a, b):
    return jnp.dot(a, b, preferred_element_type=F32, precision=HIGHEST)


def _rmsnorm_kernel(x_ref, g_ref, o_ref):
    x = x_ref[...]
    ms = jnp.mean(x * x, axis=-1, keepdims=True)
    o_ref[...] = (x * lax.rsqrt(ms + EPS) * g_ref[...]).astype(o_ref.dtype)


def _rmsnorm(x, gain, out_dtype):
    m, d = x.shape
    tm = _pick(m, 512, SUBLANE)
    return pl.pallas_call(
        _rmsnorm_kernel,
        grid=(m // tm,),
        in_specs=[pl.BlockSpec((tm, d), lambda i: (i, 0)), pl.BlockSpec((1, d), lambda i: (0, 0))],
        out_specs=pl.BlockSpec((tm, d), lambda i: (i, 0)),
        out_shape=jax.ShapeDtypeStruct((m, d), out_dtype),
        compiler_params=_params(("parallel",), 4 * tm * d * 6),
        name="rmsnorm",
    )(x, gain.reshape(1, d))


def _mm_kernel(a_ref, w_ref, o_ref, *, gelu):
    y = _dot(a_ref[...], w_ref[0])
    if gelu:
        y = 0.5 * y * (1.0 + lax.erf(y * math.sqrt(0.5)))
    o_ref[...] = y.astype(o_ref.dtype)


def _mm(a, w, layer, col0, n, out_dtype, gelu=False, name="mm"):
    m, k = a.shape
    osz = jnp.dtype(out_dtype).itemsize
    tm = _pick(m, 1024, SUBLANE)
    tn = _pick(math.gcd(n, col0) if col0 else n, 1024, LANE)
    est = lambda tn_: 2 * (tm * k * 2 + k * tn_ * 2 + tm * tn_ * osz) + tm * tn_ * 4
    while est(tn) > VMEM_BUDGET_BYTES and tn % (2 * LANE) == 0:
        tn //= 2
    jb = col0 // tn
    return pl.pallas_call(
        functools.partial(_mm_kernel, gelu=gelu),
        grid=(m // tm, n // tn),
        in_specs=[pl.BlockSpec((tm, k), lambda i, j: (i, 0)), pl.BlockSpec((1, k, tn), lambda i, j: (layer, 0, jb + j))],
        out_specs=pl.BlockSpec((tm, tn), lambda i, j: (i, j)),
        out_shape=jax.ShapeDtypeStruct((m, n), out_dtype),
        compiler_params=_params(("parallel", "parallel"), est(tn)),
        name=name,
    )(a, w)


def _ffn_up_kernel(a_ref, wg_ref, wu_ref, o_ref):
    a = a_ref[...]
    g = _dot(a, wg_ref[0])
    u = _dot(a, wu_ref[0])
    o_ref[...] = (g * jax.nn.sigmoid(g) * u).astype(o_ref.dtype)


def _ffn_up(a, w_up, layer):
    m, k = a.shape
    n = w_up.shape[2] // 2
    tn = _pick(n, 512, LANE)
    nj = n // tn
    tm = _pick(m, 2048 if tn <= 256 else 1024, SUBLANE)
    est = 2 * (tm * k * 2 + 2 * k * tn * 2 + tm * tn * 2) + 3 * tm * tn * 4
    return pl.pallas_call(
        _ffn_up_kernel,
        grid=(m // tm, nj),
        in_specs=[pl.BlockSpec((tm, k), lambda i, j: (i, 0)), pl.BlockSpec((1, k, tn), lambda i, j: (layer, 0, j)),
                  pl.BlockSpec((1, k, tn), lambda i, j: (layer, 0, nj + j))],
        out_specs=pl.BlockSpec((tm, tn), lambda i, j: (i, j)),
        out_shape=jax.ShapeDtypeStruct((m, n), BF16),
        compiler_params=_params(("parallel", "parallel"), est),
        name="ffn_up",
    )(a, w_up, w_up)


def _mm_res_kernel(a_ref, w_ref, r_ref, o_ref, *, scale):
    o_ref[...] = r_ref[...] + scale * _dot(a_ref[...], w_ref[0])


def _mm_res(a, w, layer, res, scale, name="mm_res"):
    m, k = a.shape
    n = w.shape[2]
    tm = _pick(m, 1024, SUBLANE)
    a_bufs = 2 if 2 * tm * k * 2 <= VMEM_BUDGET_BYTES // 2 else 1
    tn = _pick(n, 1024, LANE)
    est = lambda tn_: a_bufs * tm * k * 2 + 2 * (k * tn_ * 2 + 2 * tm * tn_ * 4) + tm * tn_ * 4
    while est(tn) > VMEM_BUDGET_BYTES and tn % (2 * LANE) == 0:
        tn //= 2
    a_spec = pl.BlockSpec((tm, k), lambda i, j: (i, 0), pipeline_mode=pl.Buffered(a_bufs))
    return pl.pallas_call(
        functools.partial(_mm_res_kernel, scale=scale),
        grid=(m // tm, n // tn),
        in_specs=[a_spec, pl.BlockSpec((1, k, tn), lambda i, j: (layer, 0, j)),
                  pl.BlockSpec((tm, tn), lambda i, j: (i, j))],
        out_specs=pl.BlockSpec((tm, tn), lambda i, j: (i, j)),
        out_shape=jax.ShapeDtypeStruct((m, n), F32),
        compiler_params=_params(("parallel", "arbitrary"), est(tn)),
        name=name,
    )(a, w, res)


def _merge_kernel(h_ref, y_ref, wg_ref, wb_ref, o_ref, acc_ref):
    br = pl.program_id(2)
    gate = jax.nn.sigmoid(_dot(h_ref[...], wg_ref[0]))
    term = gate * _dot(y_ref[0], wb_ref[0, 0])

    @pl.when(br == 0)
    def _():
        acc_ref[...] = term

    @pl.when(br == 1)
    def _():
        acc_ref[...] += term

    @pl.when(br == N_BRANCH - 1)
    def _():
        o_ref[...] = (acc_ref[...] + term).astype(o_ref.dtype)


def _merge(h, ys, w_gate, w_branch, layer):
    m, d = h.shape
    wdt = ys.shape[2]
    tm = _pick(m, 1024, SUBLANE)
    tn = _pick(d, 512, LANE)
    nj = d // tn
    est = 2 * (tm * d * 2 + tm * wdt * 2 + d * tn * 2 + wdt * tn * 2 + tm * tn * 2) + 4 * tm * tn * 4
    return pl.pallas_call(
        _merge_kernel,
        grid=(m // tm, nj, N_BRANCH),
        in_specs=[pl.BlockSpec((tm, d), lambda i, j, b: (i, 0)),
                  pl.BlockSpec((1, tm, wdt), lambda i, j, b: (b, i, 0)),
                  pl.BlockSpec((1, d, tn), lambda i, j, b: (layer, 0, b * nj + j)),
                  pl.BlockSpec((1, 1, wdt, tn), lambda i, j, b: (layer, b, 0, j))],
        out_specs=pl.BlockSpec((tm, tn), lambda i, j, b: (i, j)),
        out_shape=jax.ShapeDtypeStruct((m, d), BF16),
        scratch_shapes=[pltpu.VMEM((tm, tn), F32)],
        compiler_params=_params(("parallel", "parallel", "arbitrary"), est),
        name="merge",
    )(h, ys, w_gate, w_branch)


def _gmlp_kernel(u_ref, v_ref, g_ref, ws_ref, bt_ref, o_ref):
    v = v_ref[...]
    ms = jnp.mean(v * v, axis=-1, keepdims=True)
    vn = (v * lax.rsqrt(ms + EPS) * g_ref[...]).astype(BF16)
    rows = v.shape[0]
    for c in range(rows // A_CHUNK):
        r0 = c * A_CHUNK
        for g in range(A_GROUPS):
            c0 = g * A_GROUP_DIM
            s = _dot(ws_ref[0, g], vn[r0:r0 + A_CHUNK, c0:c0 + A_GROUP_DIM]) + bt_ref[:, g:g + 1]
            o_ref[r0:r0 + A_CHUNK, c0:c0 + A_GROUP_DIM] = (
                u_ref[r0:r0 + A_CHUNK, c0:c0 + A_GROUP_DIM] * s).astype(o_ref.dtype)


def _gmlp(guv, v_gain, w_s, layer, b_s):
    m = guv.shape[0]
    tm = _pick(m, 256, A_CHUNK)
    est = 2 * (2 * tm * A_WIDTH * 4 + tm * A_WIDTH * 2) + 3 * tm * A_WIDTH * 4
    return pl.pallas_call(
        _gmlp_kernel,
        grid=(m // tm,),
        in_specs=[pl.BlockSpec((tm, A_WIDTH), lambda i: (i, 0)), pl.BlockSpec((tm, A_WIDTH), lambda i: (i, 1)),
                  pl.BlockSpec((1, A_WIDTH), lambda i: (0, 0)),
                  pl.BlockSpec((1, A_GROUPS, A_CHUNK, A_CHUNK), lambda i: (layer, 0, 0, 0)),
                  pl.BlockSpec((A_CHUNK, A_GROUPS), lambda i: (0, 0))],
        out_specs=pl.BlockSpec((tm, A_WIDTH), lambda i: (i, 0)),
        out_shape=jax.ShapeDtypeStruct((m, A_WIDTH), BF16),
        compiler_params=_params(("parallel",), est),
        name="gmlp",
    )(guv, guv, v_gain.reshape(1, A_WIDTH), w_s, b_s.T)


def _gdn_gate_kernel(s_ref, alog_ref, dtb_ref, gc_ref, beta_ref):
    x = s_ref[...]
    a = x[:, C_ROPE:C_ROPE + 2 * B_HEADS]
    b = x[:, C_ROPE + 2 * B_HEADS:C_ROPE + 4 * B_HEADS]
    z = a + dtb_ref[...]
    softplus = jnp.maximum(z, 0.0) + jnp.log1p(jnp.exp(-jnp.abs(z)))
    g = -jnp.exp(alog_ref[...]) * softplus
    beta_ref[...] = jax.nn.sigmoid(b)
    ri = lax.broadcasted_iota(jnp.int32, (B_CHUNK, B_CHUNK), 0)
    ci = lax.broadcasted_iota(jnp.int32, (B_CHUNK, B_CHUNK), 1)
    tril = (ri >= ci).astype(F32)
    triu = (ri <= ci).astype(F32)
    fwd = lax.broadcasted_iota(jnp.int32, (B_CHUNK, 2 * B_HEADS), 1) < B_HEADS
    for c in range(x.shape[0] // B_CHUNK):
        gcnk = g[c * B_CHUNK:(c + 1) * B_CHUNK]
        gc_ref[c * B_CHUNK:(c + 1) * B_CHUNK, :] = jnp.where(fwd, _dot_f32(tril, gcnk), _dot_f32(triu, gcnk))


def _gdn_gate(dkv_small, a_log, dt_bias):
    m = dkv_small.shape[0]
    tm = _pick(m, 512, B_CHUNK)
    nh = 2 * B_HEADS
    small_blk = C_KV_LORA // LANE
    return pl.pallas_call(
        _gdn_gate_kernel,
        grid=(m // tm,),
        in_specs=[pl.BlockSpec((tm, LANE), lambda i: (i, small_blk)), pl.BlockSpec((1, nh), lambda i: (0, 0)),
                  pl.BlockSpec((1, nh), lambda i: (0, 0))],
        out_specs=[pl.BlockSpec((tm, nh), lambda i: (i, 0)), pl.BlockSpec((tm, nh), lambda i: (i, 0))],
        out_shape=[jax.ShapeDtypeStruct((m, nh), F32), jax.ShapeDtypeStruct((m, nh), F32)],
        compiler_params=_params(("parallel",), 1 << 20),
        name="gdn_gate",
    )(dkv_small, a_log.reshape(1, nh), dt_bias.reshape(1, nh))


def _conv_kernel(x_ref, p_ref, n_ref, w_ref, o_ref, xe_ref, *, ts, nblk):
    i = pl.program_id(1)
    j = pl.program_id(2)
    halo = SUBLANE
    xe_ref[0:halo, :] = jnp.where(i > 0, p_ref[0], 0.0)
    xe_ref[halo:halo + ts, :] = x_ref[0]
    xe_ref[halo + ts:halo + ts + halo, :] = jnp.where(i < nblk - 1, n_ref[0], 0.0)
    pad = (B_CONV - 1) // 2
    acc = None
    for t in range(B_CONV):
        term = w_ref[t:t + 1, :] * xe_ref[pl.ds(halo - pad + t, ts), :]
        acc = term if acc is None else acc + term
    y = acc * jax.nn.sigmoid(acc)
    qscale = jnp.where(j == 0, B_HEAD_DIM ** -0.5, 1.0).astype(F32)
    for h in range(B_HEADS):
        c0 = h * B_HEAD_DIM
        yh = y[:, c0:c0 + B_HEAD_DIM]
        ss = jnp.sum(yh * yh, axis=-1, keepdims=True)
        nrm = jnp.where(j < 2, lax.rsqrt(ss + EPS) * qscale, 1.0)
        o_ref[0, :, c0:c0 + B_HEAD_DIM] = yh * nrm


def _conv_qkv(qkvz, conv_w, batch, seq):
    x = qkvz.reshape(batch, seq, 4 * B_WIDTH)
    ts = _pick(seq, 256, SUBLANE)
    nblk = seq // ts
    hb = ts // SUBLANE
    nh8 = seq // SUBLANE
    tc = B_WIDTH
    est = 2 * (2 * ts * tc * 4 + 2 * SUBLANE * tc * 4) + 4 * (ts + 16) * tc * 4
    return pl.pallas_call(
        functools.partial(_conv_kernel, ts=ts, nblk=nblk),
        grid=(batch, nblk, 3),
        in_specs=[pl.BlockSpec((1, ts, tc), lambda b, i, j: (b, i, j)),
                  pl.BlockSpec((1, SUBLANE, tc), lambda b, i, j: (b, jnp.maximum(i * hb - 1, 0), j)),
                  pl.BlockSpec((1, SUBLANE, tc), lambda b, i, j: (b, jnp.minimum((i + 1) * hb, nh8 - 1), j)),
                  pl.BlockSpec((B_CONV, tc), lambda b, i, j: (0, j))],
        out_specs=pl.BlockSpec((1, ts, tc), lambda b, i, j: (b, i, j)),
        out_shape=jax.ShapeDtypeStruct((batch, seq, 3 * B_WIDTH), F32),
        scratch_shapes=[pltpu.VMEM((ts + 2 * SUBLANE, tc), F32)],
        compiler_params=_params(("parallel", "parallel", "parallel"), est),
        name="gdn_conv",
    )(x, x, x, conv_w)


def _gdn_kernel(qf_ref, kf_ref, vf_ref, gf_ref, bf_ref, gtf_ref,
                qb_ref, kb_ref, vb_ref, gb_ref, bb_ref, gtb_ref,
                of_ref, ob_ref, sf_ref, sb_ref, *, ngroup):
    h = pl.program_id(1)
    i = pl.program_id(2)

    @pl.when(i == 0)
    def _():
        sf_ref[...] = jnp.zeros_like(sf_ref)
        sb_ref[...] = jnp.zeros_like(sb_ref)

    g_rows, c_rows, dk = B_GROUP, B_CHUNK, B_HEAD_DIM
    cpg = g_rows // c_rows
    rid = lax.broadcasted_iota(jnp.int32, (g_rows, g_rows), 0)
    cid = lax.broadcasted_iota(jnp.int32, (g_rows, g_rows), 1)
    same_chunk = (rid // c_rows) == (cid // c_rows)
    eye = (rid == cid).astype(F32)
    lane = lax.broadcasted_iota(jnp.int32, (1, 2 * B_HEADS), 1)
    dirs = (
        dict(q=qf_ref, k=kf_ref, v=vf_ref, g=gf_ref, b=bf_ref, gt=gtf_ref, o=of_ref, s=sf_ref,
             col=h, incl=jnp.logical_and(same_chunk, rid >= cid), strict=jnp.logical_and(same_chunk, rid > cid),
             last=c_rows - 1, rev=False),
        dict(q=qb_ref, k=kb_ref, v=vb_ref, g=gb_ref, b=bb_ref, gt=gtb_ref, o=ob_ref, s=sb_ref,
             col=h + B_HEADS, incl=jnp.logical_and(same_chunk, rid <= cid),
             strict=jnp.logical_and(same_chunk, rid < cid), last=0, rev=True),
    )

    units = []
    for d in dirs:
        sel = lane == d["col"]
        grow_all = d["gt"][0, pl.ds(d["col"], 1), :]
        order = range(ngroup - 1, -1, -1) if d["rev"] else range(ngroup)
        for gi in order:
            r0 = gi * g_rows
            q = d["q"][0, r0:r0 + g_rows, :]
            k = d["k"][0, r0:r0 + g_rows, :]
            v = d["v"][0, r0:r0 + g_rows, :]
            gcol = jnp.sum(jnp.where(sel, d["g"][0, r0:r0 + g_rows, :], 0.0), axis=-1, keepdims=True)
            bcol = jnp.sum(jnp.where(sel, d["b"][0, r0:r0 + g_rows, :], 0.0), axis=-1, keepdims=True)
            grow = grow_all[:, r0:r0 + g_rows]
            eg = jnp.exp(gcol)
            kbeta = k * bcol
            decay = jnp.where(d["incl"], jnp.exp(jnp.where(d["incl"], gcol - grow, 0.0)), 0.0)
            x = lax.dot_general(jnp.concatenate([kbeta, q], axis=0).astype(BF16), k.astype(BF16),
                                (((1,), (1,)), ((), ())), preferred_element_type=F32)
            n = -jnp.where(d["strict"], x[0:g_rows] * decay, 0.0)
            units.append(dict(d=d, r0=r0, q=q, k=k, eg=eg, gcol=gcol, n=n, attn=x[g_rows:] * decay,
                              rhs=jnp.concatenate([v * bcol, kbeta * eg], axis=1)))

    for u in units:
        u["nb"] = u["n"].astype(BF16)
        u["p"] = eye + u["n"]
    for _ in range(int(math.log2(c_rows)) - 1):
        for u in units:
            u["nb"] = _dot(u["nb"], u["nb"]).astype(BF16)
        for u in units:
            u["p"] = u["p"] + _dot(u["p"].astype(BF16), u["nb"])

    for u in units:
        u["uw"] = _dot(u["p"].astype(BF16), u["rhs"].astype(BF16)).astype(BF16)
    for u in units:
        au = _dot(u["attn"].astype(BF16), u["uw"])
        u["ob"] = au[:, 0:dk]
        u["qeff"] = u["q"] * u["eg"] - au[:, dk:2 * dk]

    steps = {False: [], True: []}
    for u in units:
        d = u["d"]
        corder = range(cpg - 1, -1, -1) if d["rev"] else range(cpg)
        for c in corder:
            c0 = c * c_rows
            gcolc = u["gcol"][c0:c0 + c_rows]
            glc = gcolc[d["last"]:d["last"] + 1, :]
            k_dec = (u["k"][c0:c0 + c_rows] * jnp.exp(glc - gcolc)).astype(BF16)
            kuw = lax.dot_general(k_dec, u["uw"][c0:c0 + c_rows, :], (((0,), (0,)), ((), ())),
                                  preferred_element_type=F32)
            lhs = jnp.concatenate([-kuw[:, dk:2 * dk], u["qeff"][c0:c0 + c_rows]], axis=0).astype(BF16)
            steps[d["rev"]].append(dict(lhs=lhs, bmat=kuw[:, 0:dk], egl=jnp.exp(glc),
                                        ob=u["ob"][c0:c0 + c_rows], row=u["r0"] + c0))

    state = {False: sf_ref[...], True: sb_ref[...]}
    for t in range(ngroup * cpg):
        for d in dirs:
            st = steps[d["rev"]][t]
            ys = _dot(st["lhs"], state[d["rev"]].astype(BF16))
            state[d["rev"]] = state[d["rev"]] * st["egl"] + ys[0:dk] + st["bmat"]
            d["o"][0, st["row"]:st["row"] + c_rows, :] = ys[dk:dk + c_rows] + st["ob"]
    sf_ref[...] = state[False]
    sb_ref[...] = state[True]


def _gdn(qkvn, gc, beta):
    batch, seq, _ = qkvn.shape
    rows = _pick(seq, 1024, B_GROUP)
    nb = seq // rows
    nh = 2 * B_HEADS
    hd = B_HEAD_DIM
    gc_t = jnp.transpose(gc, (0, 2, 1))

    def blk(i, rev):
        return nb - 1 - i if rev else i

    in_specs = []
    for rev in (False, True):
        col = lambda c0, rev=rev: pl.BlockSpec((1, rows, hd), lambda b, h, i: (b, blk(i, rev), c0 + h))
        gsp = pl.BlockSpec((1, rows, nh), lambda b, h, i, rev=rev: (b, blk(i, rev), 0))
        in_specs += [col(0), col(B_HEADS), col(2 * B_HEADS), gsp, gsp,
                     pl.BlockSpec((1, nh, rows), lambda b, h, i, rev=rev: (b, 0, blk(i, rev)))]
    out_specs = [pl.BlockSpec((1, rows, hd), lambda b, h, i: (b, i, h)),
                 pl.BlockSpec((1, rows, hd), lambda b, h, i: (b, nb - 1 - i, h))]
    out_sd = jax.ShapeDtypeStruct((batch, seq, B_WIDTH), F32)
    args = (qkvn, qkvn, qkvn, gc, beta, gc_t)
    return pl.pallas_call(
        functools.partial(_gdn_kernel, ngroup=rows // B_GROUP),
        grid=(batch, B_HEADS, nb),
        in_specs=in_specs,
        out_specs=out_specs,
        out_shape=[out_sd, out_sd],
        scratch_shapes=[pltpu.VMEM((hd, hd), F32), pltpu.VMEM((hd, hd), F32)],
        compiler_params=_params(("parallel", "parallel", "arbitrary"), 24 << 20),
        name="gdn_scan",
    )(*args, *args)


def _gdn_out_kernel(of_ref, ob_ref, z_ref, g_ref, o_ref):
    for h in range(B_HEADS):
        c0 = h * B_HEAD_DIM
        o = of_ref[:, c0:c0 + B_HEAD_DIM] + ob_ref[:, c0:c0 + B_HEAD_DIM]
        ms = jnp.mean(o * o, axis=-1, keepdims=True)
        z = z_ref[:, c0:c0 + B_HEAD_DIM]
        o_ref[:, c0:c0 + B_HEAD_DIM] = (o * lax.rsqrt(ms + EPS) * g_ref[...] * (z * jax.nn.sigmoid(z))).astype(o_ref.dtype)


def _gdn_out(o_f, o_b, qkvz, o_gain):
    m = o_f.shape[0]
    tm = _pick(m, 512, SUBLANE)
    blk = lambda j: pl.BlockSpec((tm, B_WIDTH), lambda i: (i, j))
    return pl.pallas_call(
        _gdn_out_kernel,
        grid=(m // tm,),
        in_specs=[blk(0), blk(0), blk(3), pl.BlockSpec((1, B_HEAD_DIM), lambda i: (0, 0))],
        out_specs=blk(0),
        out_shape=jax.ShapeDtypeStruct((m, B_WIDTH), BF16),
        compiler_params=_params(("parallel",), 2 * tm * B_WIDTH * 14 + 4 * tm * B_WIDTH * 4),
        name="gdn_out",
    )(o_f, o_b, qkvz, o_gain.reshape(1, B_HEAD_DIM))


def _rope_tables_kernel(pos_ref, f_ref, cos_ref, sin_ref):
    ang = pos_ref[...].astype(F32) * f_ref[...]
    cos_ref[...] = jnp.cos(ang)
    sin_ref[...] = jnp.sin(ang)


def _rope_tables(positions):
    m = positions.size
    inv_freq = ROPE_THETA ** (-jnp.arange(0, C_ROPE, 2, dtype=F32) / C_ROPE)
    f = jnp.tile(inv_freq, LANE // (C_ROPE // 2)).reshape(1, LANE)
    tm = _pick(m, 1024, SUBLANE)
    sd = jax.ShapeDtypeStruct((m, LANE), F32)
    return pl.pallas_call(
        _rope_tables_kernel,
        grid=(m // tm,),
        in_specs=[pl.BlockSpec((tm, 1), lambda i: (i, 0)), pl.BlockSpec((1, LANE), lambda i: (0, 0))],
        out_specs=[pl.BlockSpec((tm, LANE), lambda i: (i, 0))] * 2,
        out_shape=[sd, sd],
        compiler_params=_params(("parallel",), 4 << 20),
        name="rope_tables",
    )(positions.reshape(m, 1), f)


def _rope_slab(t, cos, sin):
    half = C_ROPE // 2
    lane = lax.broadcasted_iota(jnp.int32, t.shape, 1)
    lo = lane < half
    mid = jnp.logical_and(lane >= half, lane < C_ROPE)
    t2_to_lo = pltpu.roll(t, LANE - half, 1)
    t1_to_mid = pltpu.roll(t, half, 1)
    return jnp.where(lo, t * cos - t2_to_lo * sin, jnp.where(mid, t1_to_mid * sin + t * cos, 0.0))


def _q_proj_kernel(a_ref, g_ref, w_ref, cos_ref, sin_ref, o_ref, *, scale):
    x = a_ref[...]
    ms = jnp.mean(x * x, axis=-1, keepdims=True)
    xn = (x * lax.rsqrt(ms + EPS) * g_ref[...]).astype(BF16)
    q = _dot(xn, w_ref[0])
    cos, sin = cos_ref[...], sin_ref[...]
    for h in range(o_ref.shape[1]):
        c0 = h * C_QK_PAD
        qr = _rope_slab(q[:, c0 + C_NOPE:c0 + C_QK_PAD], cos, sin)
        o_ref[0, h, :, 0:C_NOPE] = (q[:, c0:c0 + C_NOPE] * scale).astype(o_ref.dtype)
        o_ref[0, h, :, C_NOPE:C_QK_PAD] = (qr * scale).astype(o_ref.dtype)


def _q_proj(dq, gain, w_uq, layer, cos, sin, batch, seq):
    k = dq.shape[1]
    tm = _pick(seq, 1024, SUBLANE)
    spb = seq // tm
    hpt = 4
    tn = hpt * C_QK_PAD
    scale = (C_NOPE + C_ROPE) ** -0.5 * math.log2(math.e)
    est = 2 * (tm * k * 4 + k * tn * 2 + tm * tn * 2 + 2 * tm * LANE * 4) + tm * k * 6 + 3 * tm * tn * 4
    return pl.pallas_call(
        functools.partial(_q_proj_kernel, scale=scale),
        grid=(batch * spb, C_HEADS // hpt),
        in_specs=[pl.BlockSpec((tm, k), lambda i, j: (i, 0)), pl.BlockSpec((1, k), lambda i, j: (0, 0)),
                  pl.BlockSpec((1, k, tn), lambda i, j: (layer, 0, j)),
                  pl.BlockSpec((tm, LANE), lambda i, j: (i, 0)), pl.BlockSpec((tm, LANE), lambda i, j: (i, 0))],
        out_specs=pl.BlockSpec((1, hpt, tm, C_QK_PAD), lambda i, j: (i // spb, j, i % spb, 0)),
        out_shape=jax.ShapeDtypeStruct((batch, C_HEADS, seq, C_QK_PAD), BF16),
        compiler_params=_params(("parallel", "parallel"), est),
        name="mla_q_proj",
    )(dq, gain.reshape(1, k), w_uq, cos, sin)


def _kv_proj_kernel(a_ref, g_ref, w_ref, kr_ref, cos_ref, sin_ref, kt_ref, v1_ref):
    x = a_ref[...]
    ms = jnp.mean(x * x, axis=-1, keepdims=True)
    xn = (x * lax.rsqrt(ms + EPS) * g_ref[...]).astype(BF16)
    kv = _dot(xn, w_ref[0])
    rows = kv.shape[0]
    lane = lax.broadcasted_iota(jnp.int32, (rows, LANE), 1)
    kr = _rope_slab(jnp.where(lane < C_ROPE, kr_ref[...], 0.0), cos_ref[...], sin_ref[...])
    krt = jnp.transpose(kr).astype(kt_ref.dtype)
    ones = jnp.ones((rows, LANE), v1_ref.dtype)
    for h in range(kt_ref.shape[1]):
        c0 = h * (C_NOPE + C_V)
        kt_ref[0, h, 0:C_NOPE, :] = jnp.transpose(kv[:, c0:c0 + C_NOPE]).astype(kt_ref.dtype)
        kt_ref[0, h, C_NOPE:C_QK_PAD, :] = krt
        v1_ref[0, h, :, 0:C_V] = kv[:, c0 + C_NOPE:c0 + C_NOPE + C_V].astype(v1_ref.dtype)
        v1_ref[0, h, :, C_V:C_V + LANE] = ones


def _kv_proj(dkv_small, gain, w_ukv, layer, cos, sin, batch, seq):
    k = w_ukv.shape[1]
    tm = _pick(seq, 1024, LANE)
    spb = seq // tm
    hpt = 4
    tn = hpt * (C_NOPE + C_V)
    small_blk = k // LANE
    est = 2 * (tm * k * 4 + k * tn * 2 + 2 * tm * tn * 2 + 3 * tm * LANE * 4) + tm * k * 6 + 3 * tm * tn * 4
    return pl.pallas_call(
        _kv_proj_kernel,
        grid=(batch * spb, C_HEADS // hpt),
        in_specs=[pl.BlockSpec((tm, k), lambda i, j: (i, 0)), pl.BlockSpec((1, k), lambda i, j: (0, 0)),
                  pl.BlockSpec((1, k, tn), lambda i, j: (layer, 0, j)),
                  pl.BlockSpec((tm, LANE), lambda i, j: (i, small_blk)),
                  pl.BlockSpec((tm, LANE), lambda i, j: (i, 0)), pl.BlockSpec((tm, LANE), lambda i, j: (i, 0))],
        out_specs=[pl.BlockSpec((1, hpt, C_QK_PAD, tm), lambda i, j: (i // spb, j, 0, i % spb)),
                   pl.BlockSpec((1, hpt, tm, C_V + LANE), lambda i, j: (i // spb, j, i % spb, 0))],
        out_shape=[jax.ShapeDtypeStruct((batch, C_HEADS, C_QK_PAD, seq), BF16),
                   jax.ShapeDtypeStruct((batch, C_HEADS, seq, C_V + LANE), BF16)],
        compiler_params=_params(("parallel", "parallel"), est),
        name="mla_kv_proj",
    )(dkv_small, gain.reshape(1, k), w_ukv, dkv_small, cos, sin)


def _attn_kernel(q_ref, kt_ref, v1_ref, o_ref, s_ref, m_ref, acc_ref, *, tq, tkv, nsub):
    seq = q_ref.shape[2]
    ts = tq // nsub
    nq = seq // tq
    nkv = seq // tkv
    rep = tkv // LANE

    def scores_into(slot, q0, blk):
        k0 = pl.multiple_of(blk * tkv, tkv)
        kt = kt_ref[0, 0, :, pl.ds(k0, tkv)]
        for s in range(nsub):
            s_ref[slot, s] = _dot(q_ref[0, 0, pl.ds(pl.multiple_of(q0 + s * ts, ts), ts), :], kt)

    def consume(slot, blk):
        k0 = pl.multiple_of(blk * tkv, tkv)
        v1 = v1_ref[0, 0, pl.ds(k0, tkv), :]
        for s in range(nsub):
            sc = s_ref[slot, s]
            m = m_ref[s]
            m_new = jnp.maximum(m, jnp.max(sc, axis=1, keepdims=True))
            alpha = jnp.exp2(m - m_new)
            p = jnp.exp2(sc - jnp.tile(m_new, (1, rep))).astype(BF16)
            acc_ref[s] = jnp.tile(alpha, (1, 2)) * acc_ref[s] + _dot(p, v1)
            m_ref[s] = m_new

    def pair(q0, b0):
        scores_into(1, q0, b0 + 1)
        consume(0, b0)
        scores_into(0, q0, b0 + 2)
        consume(1, b0 + 1)

    def qtile(qi, carry):
        q0 = qi * tq
        q_next = jnp.minimum(qi + 1, nq - 1) * tq
        m_ref[...] = jnp.full(m_ref.shape, -jnp.inf, F32)
        acc_ref[...] = jnp.zeros(acc_ref.shape, F32)
        if nkv % 2 == 0:
            lax.fori_loop(0, nkv // 2 - 1, lambda jj, c: (pair(q0, 2 * jj), c)[1], 0)
            scores_into(1, q0, nkv - 1)
            consume(0, nkv - 2)
            scores_into(0, q_next, 0)
            consume(1, nkv - 1)
        else:
            lax.fori_loop(0, nkv // 2, lambda jj, c: (pair(q0, 2 * jj), c)[1], 0)
            consume(0, nkv - 1)
            scores_into(0, q_next, 0)
        for s in range(nsub):
            acc = acc_ref[s]
            o_ref[0, pl.ds(pl.multiple_of(q0 + s * ts, ts), ts), :] = (
                acc[:, 0:C_V] / acc[:, C_V:C_V + LANE]).astype(o_ref.dtype)
        return carry

    scores_into(0, 0, 0)
    lax.fori_loop(0, nq, qtile, 0)


def _attention(q, kt, v1):
    batch, heads, seq, _ = q.shape
    tq = _pick(seq, 1024, LANE)
    nsub = 2 if tq % (2 * LANE) == 0 else 1
    ts = tq // nsub
    tkv = _pick(seq, 1024, LANE)
    est = 2 * (3 * seq * C_QK_PAD * 2 + seq * C_V * 2) + 8 * tkv * tq * 4
    return pl.pallas_call(
        functools.partial(_attn_kernel, tq=tq, tkv=tkv, nsub=nsub),
        grid=(batch, heads),
        in_specs=[pl.BlockSpec((1, 1, seq, C_QK_PAD), lambda b, h: (b, h, 0, 0)),
                  pl.BlockSpec((1, 1, C_QK_PAD, seq), lambda b, h: (b, h, 0, 0)),
                  pl.BlockSpec((1, 1, seq, C_V + LANE), lambda b, h: (b, h, 0, 0))],
        out_specs=pl.BlockSpec((1, seq, C_V), lambda b, h: (b, 0, h)),
        out_shape=jax.ShapeDtypeStruct((batch, seq, heads * C_V), BF16),
        scratch_shapes=[pltpu.VMEM((2, nsub, ts, tkv), F32), pltpu.VMEM((nsub, ts, LANE), F32),
                        pltpu.VMEM((nsub, ts, C_V + LANE), F32)],
        compiler_params=_params(("parallel", "parallel"), est),
        name="mla_attention",
    )(q, kt, v1)


def _prepare_weights(p):
    bf = lambda t: t.astype(BF16)
    w_in = p["w_in"]
    c0 = 2 * A_WIDTH + 4 * B_WIDTH
    ab = w_in[:, :, c0:c0 + 4 * B_HEADS]
    dq = w_in[:, :, c0 + 4 * B_HEADS:c0 + 4 * B_HEADS + C_Q_LORA]
    dkv_kr = w_in[:, :, c0 + 4 * B_HEADS + C_Q_LORA:]
    depth = w_in.shape[0]
    uq = p["c_w_uq"].reshape(depth, C_Q_LORA, C_HEADS, C_NOPE + C_ROPE)
    uq = jnp.pad(uq, ((0, 0), (0, 0), (0, 0), (0, C_QK_PAD - C_NOPE - C_ROPE)))
    return dict(
        ffn1_up=bf(p["ffn1_up"]), ffn1_down=bf(p["ffn1_down"]), ffn2_up=bf(p["ffn2_up"]), ffn2_down=bf(p["ffn2_down"]),
        w_in=bf(w_in),
        w_dq=bf(dq),
        w_dkv_small=bf(jnp.concatenate([dkv_kr, ab], axis=2)),
        w_uq=bf(uq.reshape(depth, C_Q_LORA, C_HEADS * C_QK_PAD)),
        w_ukv=bf(p["c_w_ukv"]), w_gate=bf(p["w_gate"]), w_branch=bf(p["w_branch"]), w_o=bf(p["w_o"]),
        a_w_s=bf(p["a_w_s"]))


def kernel(x, positions, norm_ffn1, ffn1_up, ffn1_down, norm_mix, w_in, w_gate, a_v_gain, a_w_s, a_b_s, b_conv,
           b_a_log, b_dt_bias, b_o_gain, c_q_gain, c_kv_gain, c_w_uq, c_w_ukv, w_branch, w_o, norm_ffn2, ffn2_up,
           ffn2_down, norm_final):
    batch, seq, d = x.shape
    depth = norm_ffn1.shape[0]
    m = batch * seq
    w = _prepare_weights(dict(ffn1_up=ffn1_up, ffn1_down=ffn1_down, ffn2_up=ffn2_up, ffn2_down=ffn2_down, w_in=w_in,
                              w_gate=w_gate, c_w_uq=c_w_uq, c_w_ukv=c_w_ukv, w_branch=w_branch, w_o=w_o, a_w_s=a_w_s))

    cos, sin = _rope_tables(positions)
    xs = x.reshape(m, d)
    for l in range(depth):
        hn = _rmsnorm(xs, norm_ffn1[l], BF16)
        xs = _mm_res(_ffn_up(hn, w["ffn1_up"], l), w["ffn1_down"], l, xs, 0.5, name="ffn_down")

        h = _rmsnorm(xs, norm_mix[l], BF16)
        guv = _mm(h, w["w_in"], l, 0, 2 * A_WIDTH, F32, gelu=True, name="proj_a")
        qkvz = _mm(h, w["w_in"], l, 2 * A_WIDTH, 4 * B_WIDTH, F32, name="proj_b")
        dq = _mm(h, w["w_dq"], l, 0, C_Q_LORA, F32, name="proj_dq")
        dkv_small = _mm(h, w["w_dkv_small"], l, 0, C_KV_LORA + LANE, F32, name="proj_dkv")

        y_a = _gmlp(guv, a_v_gain[l], w["a_w_s"], l, a_b_s[l])

        gc, beta = _gdn_gate(dkv_small, b_a_log[l], b_dt_bias[l])
        qkvn = _conv_qkv(qkvz, b_conv[l], batch, seq)
        nh = 2 * B_HEADS
        o_f, o_b = _gdn(qkvn, gc.reshape(batch, seq, nh), beta.reshape(batch, seq, nh))
        y_b = _gdn_out(o_f.reshape(m, B_WIDTH), o_b.reshape(m, B_WIDTH), qkvz, b_o_gain[l])

        qh = _q_proj(dq, c_q_gain[l], w["w_uq"], l, cos, sin, batch, seq)
        kt, v1 = _kv_proj(dkv_small, c_kv_gain[l], w["w_ukv"], l, cos, sin, batch, seq)
        y_c = _attention(qh, kt, v1).reshape(m, C_WIDTH)

        merged = _merge(h, jnp.stack([y_a, y_b, y_c], axis=0), w["w_gate"], w["w_branch"], l)
        xs = _mm_res(merged, w["w_o"], l, xs, 1.0, name="out_proj")

        hn = _rmsnorm(xs, norm_ffn2[l], BF16)
        xs = _mm_res(_ffn_up(hn, w["ffn2_up"], l), w["ffn2_down"], l, xs, 0.5, name="ffn_down")

    return _rmsnorm(xs, norm_final, F32).reshape(batch, seq, d)
```
